```python
import math
import jax, jax.numpy as jnp
from jax import lax
import numpy as np

D_MODEL = 1024
BATCH = 8
SEQ = 2048
DEPTH = 4
DEC_BATCH = 32
DEC_SEQ = 4
PAST_LEN = 16384
PAGE_SIZE = 128

MLA_HEADS = 8
Q_LORA = 384
KV_LORA = 256
NOPE_DIM = 64
ROPE_DIM = 32
V_DIM = 64
ROPE_BASE = 10000.0
Q_BLOCK = 128
SSM_HEADS = 8
SSM_HEAD_DIM = 64
SSM_INNER = SSM_HEADS * SSM_HEAD_DIM
SSM_GROUPS = 2
SSM_STATE = 128
SSM_CONV = 4
SSM_CONV_DIM = SSM_INNER + 2 * SSM_GROUPS * SSM_STATE
SSD_CHUNK = 128
MIX_AB_IN = Q_LORA + KV_LORA + ROPE_DIM + SSM_INNER + SSM_CONV_DIM + SSM_HEADS
MIX_AB_OUT = MLA_HEADS * V_DIM + SSM_INNER
CONF_CH = D_MODEL
CONF_K = 31
N_EXPERTS = 16
N_GROUPS = 4
TOP_K = 2
D_EXPERT = 256
DN_ALPHA = (2 * DEPTH) ** 0.25
DN_BETA = (8 * DEPTH) ** -0.25
EPS = 1e-6

kernel_name = 'hybrid_mla_ssd_conformer_moe_step'


def layer_norm(x, g, b):
    xf = x.astype(jnp.float32)
    mu = jnp.mean(xf, -1, keepdims=True)
    var = jnp.mean(jnp.square(xf - mu), -1, keepdims=True)
    return ((xf - mu) * lax.rsqrt(var + EPS)).astype(x.dtype) * g + b


def rms_norm(x, g):
    xf = x.astype(jnp.float32)
    return (xf * lax.rsqrt(jnp.mean(xf * xf, -1, keepdims=True) + EPS)).astype(x.dtype) * g


def rope(x, pos):
    half = x.shape[-1] // 2
    inv = ROPE_BASE ** (-jnp.arange(half, dtype=jnp.float32) / half)
    ang = pos.astype(jnp.float32)[:, None] * inv[None, :]
    cos = jnp.cos(ang)[None, :, None, :].astype(x.dtype)
    sin = jnp.sin(ang)[None, :, None, :].astype(x.dtype)
    x1, x2 = x[..., :half], x[..., half:]
    return jnp.concatenate([x1 * cos - x2 * sin, x2 * cos + x1 * sin], axis=-1)


def causal_dwconv(xp, w):
    return lax.conv_general_dilated(xp, w[:, None, :].astype(xp.dtype), (1,), 'VALID',
                                    dimension_numbers=('NWC', 'WIO', 'NWC'),
                                    feature_group_count=xp.shape[-1])


def ssd(x, dt, a, bm, cm, h0):
    b, L, H, P = x.shape
    N = bm.shape[-1]
    l = min(SSD_CHUNK, L)
    nc = L // l
    xc = x.reshape(b, nc, l, H, P)
    bc = bm.reshape(b, nc, l, H, N)
    cc = cm.reshape(b, nc, l, H, N)
    dtc = dt.reshape(b, nc, l, H)
    acs = jnp.cumsum(dtc * a, axis=2)
    seg = acs[:, :, :, None, :] - acs[:, :, None, :, :]
    causal = jnp.tril(jnp.ones((l, l), bool))[None, None, :, :, None]
    decay = jnp.exp(jnp.where(causal, seg, -jnp.inf)).astype(x.dtype)
    dtx = dtc.astype(x.dtype)[..., None] * xc
    cb = jnp.einsum('bcihn,bcjhn->bcijh', cc, bc)
    y_diag = jnp.einsum('bcijh,bcjhp->bcihp', cb * decay, dtx)
    to_end = jnp.exp(acs[:, :, -1:, :] - acs).astype(x.dtype)
    states = jnp.einsum('bclh,bclhn,bclhp->bchpn', to_end, bc, dtx).astype(h0.dtype)
    chunk_decay = jnp.exp(acs[:, :, -1, :]).astype(h0.dtype)

    def step(h, inp):
        dec, st = inp
        return dec[:, :, None, None] * h + st, h

    h_last, h_prev = lax.scan(step, h0, (jnp.swapaxes(chunk_decay, 0, 1), jnp.swapaxes(states, 0, 1)))
    h_prev = jnp.swapaxes(h_prev, 0, 1)
    y_off = jnp.einsum('bclhn,bchpn->bclhp', cc * jnp.exp(acs).astype(x.dtype)[..., None], h_prev.astype(x.dtype))
    return (y_diag + y_off).reshape(b, L, H, P), h_last


def mla_attend_prompt(q_abs, q_pe, c_kv, k_pe):
    b, S, H, R = q_abs.shape
    nb = S // Q_BLOCK
    scale = (NOPE_DIM + ROPE_DIM) ** -0.5
    qa = jnp.swapaxes(q_abs.reshape(b, nb, Q_BLOCK, H, R), 0, 1)
    qp = jnp.swapaxes(q_pe.reshape(b, nb, Q_BLOCK, H, ROPE_DIM), 0, 1)
    kpos = jnp.arange(S)

    def block(args):
        qa_i, qp_i, i = args
        sc = (jnp.einsum('bqhr,btr->bhqt', qa_i, c_kv)
              + jnp.einsum('bqhe,bte->bhqt', qp_i, k_pe)).astype(jnp.float32) * scale
        qpos = i * Q_BLOCK + jnp.arange(Q_BLOCK)
        sc = jnp.where(kpos[None, :] <= qpos[:, None], sc, -jnp.inf)
        p = jax.nn.softmax(sc, axis=-1).astype(c_kv.dtype)
        return jnp.einsum('bhqt,btr->bqhr', p, c_kv)

    o = lax.map(block, (qa, qp, jnp.arange(nb)))
    return jnp.swapaxes(o, 0, 1).reshape(b, S, H, R)


def mla_attend_sample(q_abs, q_pe, c_new, kpe_new, c_past, kpe_past):
    s = q_abs.shape[1]
    T = c_past.shape[1]
    scale = (NOPE_DIM + ROPE_DIM) ** -0.5
    sp = (jnp.einsum('bqhr,btr->bhqt', q_abs, c_past)
          + jnp.einsum('bqhe,bte->bhqt', q_pe, kpe_past)).astype(jnp.float32)
    sn = (jnp.einsum('bqhr,btr->bhqt', q_abs, c_new)
          + jnp.einsum('bqhe,bte->bhqt', q_pe, kpe_new)).astype(jnp.float32)
    sn = jnp.where(jnp.tril(jnp.ones((s, s), bool)), sn, -jnp.inf)
    p = jax.nn.softmax(jnp.concatenate([sp, sn], axis=-1) * scale, axis=-1).astype(c_new.dtype)
    return (jnp.einsum('bhqt,btr->bqhr', p[..., :T], c_past)
            + jnp.einsum('bhqt,btr->bqhr', p[..., T:], c_new))


def mixer_ab(x, pos, conv_hist, h0, past_ckv, past_kpe, w_in, g_q, g_kv, w_uq, w_uk, w_uv,
             conv_w, conv_b, dt_bias, a_log, d_skip, g_y, w_out):
    b, s, _ = x.shape
    proj = jnp.einsum('bsd,de->bse', x, w_in)
    o1 = Q_LORA
    o2 = o1 + KV_LORA
    o3 = o2 + ROPE_DIM
    o4 = o3 + SSM_INNER
    o5 = o4 + SSM_CONV_DIM
    h_q, h_kv, kpe_raw, z, xbc, dt_raw = jnp.split(proj, [o1, o2, o3, o4, o5], axis=-1)
    q = jnp.einsum('bsq,qhe->bshe', rms_norm(h_q, g_q), w_uq)
    q_pe = rope(q[..., NOPE_DIM:], pos)
    q_abs = jnp.einsum('bshn,rhn->bshr', q[..., :NOPE_DIM], w_uk)
    c_kv = rms_norm(h_kv, g_kv)
    k_pe = rope(kpe_raw[:, :, None, :], pos)[:, :, 0, :]
    if past_ckv is None:
        o_lat = mla_attend_prompt(q_abs, q_pe, c_kv, k_pe)
    else:
        o_lat = mla_attend_sample(q_abs, q_pe, c_kv, k_pe, past_ckv, past_kpe)
    o_mla = jnp.einsum('bshr,rhv->bshv', o_lat, w_uv).reshape(b, s, MLA_HEADS * V_DIM)
    xp = jnp.concatenate([conv_hist.astype(xbc.dtype), xbc], axis=1)
    new_conv = xp[:, xp.shape[1] - (SSM_CONV - 1):]
    xbc = jax.nn.silu(causal_dwconv(xp, conv_w) + conv_b)
    xs, bm, cm = jnp.split(xbc, [SSM_INNER, SSM_INNER + SSM_GROUPS * SSM_STATE], axis=-1)
    xs = xs.reshape(b, s, SSM_HEADS, SSM_HEAD_DIM)
    rep = SSM_HEADS // SSM_GROUPS
    bm = jnp.repeat(bm.reshape(b, s, SSM_GROUPS, SSM_STATE), rep, axis=2)
    cm = jnp.repeat(cm.reshape(b, s, SSM_GROUPS, SSM_STATE), rep, axis=2)
    dt = jax.nn.softplus(dt_raw.astype(jnp.float32) + dt_bias.astype(jnp.float32))
    a = -jnp.exp(a_log.astype(jnp.float32))
    y, h_last = ssd(xs, dt, a, bm, cm, h0)
    y = (y + d_skip[:, None] * xs).reshape(b, s, SSM_INNER) * jax.nn.silu(z)
    y = rms_norm(y.reshape(b, s, SSM_GROUPS, -1), g_y.reshape(SSM_GROUPS, -1)).reshape(b, s, SSM_INNER)
    out = jnp.einsum('bse,ed->bsd', jnp.concatenate([o_mla, y], axis=-1), w_out)
    return out, c_kv, k_pe, h_last, new_conv


def mixer_c(x, hist, w_in, b_in, dw_w, dw_b, g_n, b_n, w_out, b_out):
    u = jnp.einsum('bsd,de->bse', x, w_in) + b_in
    v = u[..., :CONF_CH] * jax.nn.sigmoid(u[..., CONF_CH:])
    vp = jnp.concatenate([hist.astype(v.dtype), v], axis=1)
    new_hist = vp[:, vp.shape[1] - (CONF_K - 1):]
    c = causal_dwconv(vp, dw_w) + dw_b
    c = jax.nn.silu(layer_norm(c, g_n, b_n))
    return jnp.einsum('bsc,cd->bsd', c, w_out) + b_out, new_hist


def moe(x, w_router, router_bias, w_gate, w_up, w_down):
    probs = jax.nn.sigmoid(jnp.einsum('bsd,de->bse', x, w_router).astype(jnp.float32))
    sel = probs + router_bias.astype(jnp.float32)
    b, s, E = sel.shape
    per = E // N_GROUPS
    grp = lax.top_k(sel.reshape(b, s, N_GROUPS, per), TOP_K)[0].sum(-1)
    g_idx = jnp.argmax(grp, axis=-1)
    in_grp = (jnp.arange(E) // per)[None, None, :] == g_idx[..., None]
    _, e_idx = lax.top_k(jnp.where(in_grp, sel, -jnp.inf), TOP_K)
    w = jnp.take_along_axis(probs, e_idx, axis=-1)
    w = w / jnp.sum(w, -1, keepdims=True)
    gates = jnp.sum(jax.nn.one_hot(e_idx, E, dtype=jnp.float32) * w[..., None], axis=-2).astype(x.dtype)
    h = jax.nn.silu(jnp.einsum('bsd,edf->bsef', x, w_gate)) * jnp.einsum('bsd,edf->bsef', x, w_up)
    return jnp.einsum('bsef,efd->bsd', h * gates[..., None], w_down)


def setup_inputs(seed: int = 0) -> dict:
    key = jax.random.key(seed)
    ks = list(jax.random.split(key, 48))
    f32 = jnp.float32
    n_a = (DEPTH + 1) // 2
    n_c = DEPTH // 2
    n_pages = PAST_LEN // PAGE_SIZE
    n_used = DEC_BATCH * n_pages
    n_pool = n_used + n_used // 4

    def nrm(shape, scale):
        return jax.random.normal(ks.pop(), shape, f32) * scale

    def gain(shape):
        return 1.0 + nrm(shape, 0.02)

    x_prompt = nrm((BATCH, SEQ, D_MODEL), 1.0)
    x_sample = nrm((DEC_BATCH, DEC_SEQ, D_MODEL), 1.0)
    cache_ckv = nrm((n_a, n_pool, PAGE_SIZE, KV_LORA), 1.0)
    cache_kpe = nrm((n_a, n_pool, PAGE_SIZE, ROPE_DIM), 1.0)
    state_ssm = nrm((n_a, DEC_BATCH, SSM_HEADS, SSM_HEAD_DIM, SSM_STATE), 0.5)
    state_ssm_conv = nrm((n_a, DEC_BATCH, SSM_CONV - 1, SSM_CONV_DIM), 1.0)
    state_conf_conv = nrm((n_c, DEC_BATCH, CONF_K - 1, CONF_CH), 0.5)
    page_table = jax.random.permutation(ks.pop(), n_pool)[:n_used].reshape(DEC_BATCH, n_pages).astype(jnp.int32)
    dt0 = jnp.exp(jax.random.uniform(ks.pop(), (n_a, SSM_HEADS), f32, math.log(1e-3), math.log(1e-1)))
    ssm_dt_bias = dt0 + jnp.log(-jnp.expm1(-dt0))
    ssm_a_log = jnp.log(jax.random.uniform(ks.pop(), (n_a, SSM_HEADS), f32, 1.0, 16.0))
    return {
        'x_prompt': x_prompt,
        'x_sample': x_sample,
        'cache_ckv': cache_ckv,
        'cache_kpe': cache_kpe,
        'state_ssm': state_ssm,
        'state_ssm_conv': state_ssm_conv,
        'state_conf_conv': state_conf_conv,
        'page_table': page_table,
        'w_in_ab': nrm((n_a, D_MODEL, MIX_AB_IN), D_MODEL ** -0.5),
        'g_q_norm': gain((n_a, Q_LORA)),
        'g_kv_norm': gain((n_a, KV_LORA)),
        'w_uq': nrm((n_a, Q_LORA, MLA_HEADS, NOPE_DIM + ROPE_DIM), Q_LORA ** -0.5),
        'w_uk': nrm((n_a, KV_LORA, MLA_HEADS, NOPE_DIM), KV_LORA ** -0.5),
        'w_uv': nrm((n_a, KV_LORA, MLA_HEADS, V_DIM), KV_LORA ** -0.5),
        'ssm_conv_w': nrm((n_a, SSM_CONV, SSM_CONV_DIM), SSM_CONV ** -0.5),
        'ssm_conv_b': nrm((n_a, SSM_CONV_DIM), 0.02),
        'ssm_dt_bias': ssm_dt_bias,
        'ssm_a_log': ssm_a_log,
        'ssm_d': 1.0 + nrm((n_a, SSM_HEADS), 0.1),
        'ssm_norm_g': gain((n_a, SSM_INNER)),
        'w_out_ab': nrm((n_a, MIX_AB_OUT, D_MODEL), MIX_AB_OUT ** -0.5 * DN_BETA),
        'conf_w_in': nrm((n_c, D_MODEL, 2 * CONF_CH), D_MODEL ** -0.5),
        'conf_b_in': nrm((n_c, 2 * CONF_CH), 0.02),
        'conf_dw_w': nrm((n_c, CONF_K, CONF_CH), CONF_K ** -0.5),
        'conf_dw_b': nrm((n_c, CONF_CH), 0.02),
        'conf_norm_g': gain((n_c, CONF_CH)),
        'conf_norm_b': nrm((n_c, CONF_CH), 0.02),
        'conf_w_out': nrm((n_c, CONF_CH, D_MODEL), CONF_CH ** -0.5 * DN_BETA),
        'conf_b_out': nrm((n_c, D_MODEL), 0.02),
        'ln_mix_g': gain((DEPTH, D_MODEL)),
        'ln_mix_b': nrm((DEPTH, D_MODEL), 0.02),
        'ln_ffn_g': gain((DEPTH, D_MODEL)),
        'ln_ffn_b': nrm((DEPTH, D_MODEL), 0.02),
        'w_router': nrm((D_MODEL, N_EXPERTS), D_MODEL ** -0.5),
        'router_bias': nrm((N_EXPERTS,), 0.01),
        'moe_w_gate': nrm((DEPTH, N_EXPERTS, D_MODEL, D_EXPERT), D_MODEL ** -0.5),
        'moe_w_up': nrm((DEPTH, N_EXPERTS, D_MODEL, D_EXPERT), D_MODEL ** -0.5),
        'moe_w_down': nrm((DEPTH, N_EXPERTS, D_EXPERT, D_MODEL), D_EXPERT ** -0.5 * DN_BETA),
    }


def reference(x_prompt, x_sample, cache_ckv, cache_kpe, state_ssm, state_ssm_conv, state_conf_conv,
              page_table, w_in_ab, g_q_norm, g_kv_norm, w_uq, w_uk, w_uv, ssm_conv_w, ssm_conv_b,
              ssm_dt_bias, ssm_a_log, ssm_d, ssm_norm_g, w_out_ab, conf_w_in, conf_b_in, conf_dw_w,
              conf_dw_b, conf_norm_g, conf_norm_b, conf_w_out, conf_b_out, ln_mix_g, ln_mix_b,
              ln_ffn_g, ln_ffn_b, w_router, router_bias, moe_w_gate, moe_w_up, moe_w_down):
    bp, S, _ = x_prompt.shape
    bd, t_new, _ = x_sample.shape
    past_len = page_table.shape[1] * PAGE_SIZE
    pos_p = jnp.arange(S, dtype=jnp.int32)
    pos_s = past_len + jnp.arange(t_new, dtype=jnp.int32)
    hp, hs = x_prompt, x_sample
    ckv_p, kpe_p, ssm_p, sconv_p, cconv_p = [], [], [], [], []
    ckv_s, kpe_s, ssm_s, sconv_s, cconv_s = [], [], [], [], []
    for l in range(DEPTH):
        i = l // 2
        if l % 2 == 0:
            wts = (w_in_ab[i], g_q_norm[i], g_kv_norm[i], w_uq[i], w_uk[i], w_uv[i],
                   ssm_conv_w[i], ssm_conv_b[i], ssm_dt_bias[i], ssm_a_log[i], ssm_d[i],
                   ssm_norm_g[i], w_out_ab[i])
            zc = jnp.zeros((bp, SSM_CONV - 1, SSM_CONV_DIM), hp.dtype)
            zh = jnp.zeros((bp, SSM_HEADS, SSM_HEAD_DIM, SSM_STATE), hp.dtype)
            mp, c1, k1, h1, v1 = mixer_ab(hp, pos_p, zc, zh, None, None, *wts)
            c_past = cache_ckv[i, page_table].reshape(bd, past_len, KV_LORA)
            k_past = cache_kpe[i, page_table].reshape(bd, past_len, ROPE_DIM)
            ms, c2, k2, h2, v2 = mixer_ab(hs, pos_s, state_ssm_conv[i], state_ssm[i], c_past, k_past, *wts)
            ckv_p.append(c1); kpe_p.append(k1); ssm_p.append(h1); sconv_p.append(v1)
            ckv_s.append(c2); kpe_s.append(k2); ssm_s.append(h2); sconv_s.append(v2)
        else:
            wts = (conf_w_in[i], conf_b_in[i], conf_dw_w[i], conf_dw_b[i], conf_norm_g[i],
                   conf_norm_b[i], conf_w_out[i], conf_b_out[i])
            mp, u1 = mixer_c(hp, jnp.zeros((bp, CONF_K - 1, CONF_CH), hp.dtype), *wts)
            ms, u2 = mixer_c(hs, state_conf_conv[i], *wts)
            cconv_p.append(u1); cconv_s.append(u2)
        hp = layer_norm(DN_ALPHA * hp + mp, ln_mix_g[l], ln_mix_b[l])
        hs = layer_norm(DN_ALPHA * hs + ms, ln_mix_g[l], ln_mix_b[l])
        ffn = (w_router, router_bias, moe_w_gate[l], moe_w_up[l], moe_w_down[l])
        hp = layer_norm(DN_ALPHA * hp + moe(hp, *ffn), ln_ffn_g[l], ln_ffn_b[l])
        hs = layer_norm(DN_ALPHA * hs + moe(hs, *ffn), ln_ffn_g[l], ln_ffn_b[l])
    return (hp, hs,
            jnp.stack(ckv_p), jnp.stack(kpe_p), jnp.stack(ssm_p), jnp.stack(sconv_p), jnp.stack(cconv_p),
            jnp.stack(ckv_s), jnp.stack(kpe_s), jnp.stack(ssm_s), jnp.stack(sconv_s), jnp.stack(cconv_s))
```

```python
import functools
import math

import jax
import jax.numpy as jnp
from jax import lax
from jax.experimental import pallas as pl
from jax.experimental.pallas import tpu as pltpu

F32 = jnp.float32
BF16 = jnp.bfloat16

PAGE_SIZE = 128
NOPE_DIM = 64
ROPE_DIM = 32
ROPE_BASE = 10000.0
SSM_GROUPS = 2
SSD_CHUNK = 128
N_GROUPS = 4
TOP_K = 2
EPS = 1e-6

LANES = 128
SUBLANES = 8
VMEM_LIMIT_BYTES = 56 * 1024 * 1024
SAMPLE_ROWS = SUBLANES
PAGES_PER_STEP = 8


def _dot(a, b):
    return jnp.dot(a, b, preferred_element_type=F32)


def _dot_nt(a, b):
    return lax.dot_general(a, b, (((1,), (1,)), ((), ())), preferred_element_type=F32)


def _dot_tn(a, b):
    return lax.dot_general(a, b, (((0,), (0,)), ((), ())), preferred_element_type=F32)


def _rms(x):
    return x * lax.rsqrt(jnp.mean(x * x, axis=-1, keepdims=True) + EPS)


def _layer_norm(r, g, b):
    mu = jnp.mean(r, axis=-1, keepdims=True)
    d = r - mu
    var = jnp.mean(d * d, axis=-1, keepdims=True)
    return d * lax.rsqrt(var + EPS) * g + b


def _silu(x):
    return x * jax.nn.sigmoid(x)


def _softplus(x):
    return jnp.maximum(x, 0.0) + jnp.log1p(jnp.exp(-jnp.abs(x)))


def _split3(x):
    hi = x.astype(BF16)
    r1 = x - hi.astype(F32)
    mid = r1.astype(BF16)
    lo = (r1 - mid.astype(F32)).astype(BF16)
    return hi, mid, lo


def _cparams(sem):
    return pltpu.CompilerParams(dimension_semantics=sem, vmem_limit_bytes=VMEM_LIMIT_BYTES)


def _inproj_kernel(x_ref, w_ref, gq_ref, gkv_ref, cos_ref, sin_ref,
                   hq_ref, ckv_ref, kpe_ref, kcat_ref, z_ref, xbc_ref, dt_ref, *, cols):
    xb = x_ref[...].astype(BF16)

    def proj(name):
        a, b = cols[name]
        return _dot(xb, w_ref[:, a:b])

    hq_ref[...] = (_rms(proj("q")) * gq_ref[...]).astype(hq_ref.dtype)
    ckv = _rms(proj("kv")) * gkv_ref[...]
    ckv_ref[...] = ckv
    kpe = proj("kpa") * cos_ref[...] + proj("kpb") * sin_ref[...]
    kpe_ref[...] = kpe
    nkv = ckv.shape[-1]
    kcat_ref[:, :nkv] = ckv.astype(kcat_ref.dtype)
    kcat_ref[:, nkv:] = kpe.astype(kcat_ref.dtype)
    z_ref[...] = proj("z")
    xbc_ref[...] = proj("xbc")
    dt_ref[...] = proj("dt")


def _inproj(x, w, gq, gkv, cos_t, sin_t, cols, tm, act_dtype):
    T, D = x.shape
    n_tab = cos_t.shape[0] // tm
    widths = {k: b - a for k, (a, b) in cols.items()}
    row = lambda i: (i, 0)
    fixed = lambda i: (0, 0)
    tab = lambda i: (i % n_tab, 0)
    out_shapes = (
        jax.ShapeDtypeStruct((T, widths["q"]), act_dtype),
        jax.ShapeDtypeStruct((T, widths["kv"]), F32),
        jax.ShapeDtypeStruct((T, widths["kpa"]), F32),
        jax.ShapeDtypeStruct((T, widths["kv"] + widths["kpa"]), act_dtype),
        jax.ShapeDtypeStruct((T, widths["z"]), F32),
        jax.ShapeDtypeStruct((T, widths["xbc"]), F32),
        jax.ShapeDtypeStruct((T, widths["dt"]), F32),
    )
    return pl.pallas_call(
        functools.partial(_inproj_kernel, cols=cols),
        grid=(T // tm,),
        in_specs=[
            pl.BlockSpec((tm, D), row),
            pl.BlockSpec(w.shape, fixed),
            pl.BlockSpec(gq.shape, fixed),
            pl.BlockSpec(gkv.shape, fixed),
            pl.BlockSpec((tm, cos_t.shape[1]), tab),
            pl.BlockSpec((tm, sin_t.shape[1]), tab),
        ],
        out_specs=tuple(pl.BlockSpec((tm, s.shape[1]), row) for s in out_shapes),
        out_shape=out_shapes,
        compiler_params=_cparams(("parallel",)),
        name="inproj",
    )(x, w, gq, gkv, cos_t, sin_t)


def _qprep_kernel(hq_ref, wn_ref, wpa_ref, wpb_ref, wuk_ref, cos_ref, sin_ref, qabs_ref, qpe_ref, *, heads):
    hq = hq_ref[...].astype(BF16)
    qn = _dot(hq, wn_ref[...])
    qpe = _dot(hq, wpa_ref[...]) * cos_ref[...] + _dot(hq, wpb_ref[...]) * sin_ref[...]
    qpe_ref[...] = qpe.astype(qpe_ref.dtype)
    per = LANES // NOPE_DIM
    for h in range(heads):
        slab = qn[:, (h // per) * LANES:(h // per + 1) * LANES].astype(BF16)
        qabs_ref[h] = _dot(slab, wuk_ref[h]).astype(qabs_ref.dtype)


def _qprep(hq, wn, wpa, wpb, wuk, cos_t, sin_t, tm, act_dtype):
    T, Q = hq.shape
    heads, _, R = wuk.shape
    n_tab = cos_t.shape[0] // tm
    row = lambda i: (i, 0)
    fixed2 = lambda i: (0, 0)
    tab = lambda i: (i % n_tab, 0)
    return pl.pallas_call(
        functools.partial(_qprep_kernel, heads=heads),
        grid=(T // tm,),
        in_specs=[
            pl.BlockSpec((tm, Q), row),
            pl.BlockSpec(wn.shape, fixed2),
            pl.BlockSpec(wpa.shape, fixed2),
            pl.BlockSpec(wpb.shape, fixed2),
            pl.BlockSpec(wuk.shape, lambda i: (0, 0, 0)),
            pl.BlockSpec((tm, cos_t.shape[1]), tab),
            pl.BlockSpec((tm, sin_t.shape[1]), tab),
        ],
        out_specs=(pl.BlockSpec((heads, tm, R), lambda i: (0, i, 0)),
                   pl.BlockSpec((tm, wpa.shape[1]), row)),
        out_shape=(jax.ShapeDtypeStruct((heads, T, R), act_dtype),
                   jax.ShapeDtypeStruct((T, wpa.shape[1]), act_dtype)),
        compiler_params=_cparams(("parallel",)),
        name="qprep",
    )(hq, wn, wpa, wpb, wuk, cos_t, sin_t)


def _attn_kernel(qi_ref, kj_ref, qabs_ref, qpe_ref, k_ref, wuv_ref, o_ref,
                 q_scr, m_scr, l_scr, acc_scr, *, heads, tq, scale):
    s = pl.program_id(1)
    i = qi_ref[s]
    j = kj_ref[s]
    nkv = qabs_ref.shape[-1]

    @pl.when(j == 0)
    def _():
        qpe = qpe_ref[...]
        lane = lax.broadcasted_iota(jnp.int32, qpe.shape, 1)
        for h in range(heads):
            q_scr[h * tq:(h + 1) * tq, :nkv] = qabs_ref[h].astype(BF16)
            q_scr[h * tq:(h + 1) * tq, nkv:] = jnp.where(lane // ROPE_DIM == h, qpe, 0).astype(BF16)
        m_scr[...] = jnp.full(m_scr.shape, -jnp.inf, F32)
        l_scr[...] = jnp.zeros(l_scr.shape, F32)
        acc_scr[...] = jnp.zeros(acc_scr.shape, F32)

    k = k_ref[...]

    def update(sc):
        m_old = m_scr[...]
        m_new = jnp.maximum(m_old, jnp.max(sc, axis=-1, keepdims=True))
        alpha = jnp.exp(m_old - m_new)
        p = jnp.exp(sc - m_new)
        l_scr[...] = alpha * l_scr[...] + jnp.sum(p, axis=-1, keepdims=True)
        acc_scr[...] = alpha * acc_scr[...] + _dot(p.astype(BF16), k[:, :nkv])
        m_scr[...] = m_new

    @pl.when(j < i)
    def _():
        update(_dot_nt(q_scr[...], k) * scale)

    @pl.when(j == i)
    def _():
        sc = _dot_nt(q_scr[...], k) * scale
        r = lax.broadcasted_iota(jnp.int32, sc.shape, 0)
        c = lax.broadcasted_iota(jnp.int32, sc.shape, 1)
        update(jnp.where(c <= (r & (tq - 1)), sc, -jnp.inf))
        o = acc_scr[...] / l_scr[...]
        for p in range(heads // 2):
            lo = o[(2 * p) * tq:(2 * p + 1) * tq].astype(BF16)
            hi = o[(2 * p + 1) * tq:(2 * p + 2) * tq].astype(BF16)
            o_ref[:, p * LANES:(p + 1) * LANES] = (
                _dot(lo, wuv_ref[2 * p]) + _dot(hi, wuv_ref[2 * p + 1])).astype(o_ref.dtype)


def _attn_prompt(qabs, qpe, kcat, wuv, B, S, tq, scale, act_dtype):
    heads, T, R = qabs.shape
    nq = S // tq
    pairs = [(i, j) for i in range(nq) for j in range(i + 1)]
    qi = jnp.asarray([p[0] for p in pairs], jnp.int32)
    kj = jnp.asarray([p[1] for p in pairs], jnp.int32)
    kw = kcat.shape[1]
    out_w = (heads // 2) * LANES
    grid_spec = pltpu.PrefetchScalarGridSpec(
        num_scalar_prefetch=2,
        grid=(B, len(pairs)),
        in_specs=[
            pl.BlockSpec((heads, tq, R), lambda b, s, qi, kj: (0, b * nq + qi[s], 0)),
            pl.BlockSpec((tq, qpe.shape[1]), lambda b, s, qi, kj: (b * nq + qi[s], 0)),
            pl.BlockSpec((tq, kw), lambda b, s, qi, kj: (b * nq + kj[s], 0)),
            pl.BlockSpec(wuv.shape, lambda b, s, qi, kj: (0, 0, 0)),
        ],
        out_specs=pl.BlockSpec((tq, out_w), lambda b, s, qi, kj: (b * nq + qi[s], 0)),
        scratch_shapes=[
            pltpu.VMEM((heads * tq, kw), BF16),
            pltpu.VMEM((heads * tq, 1), F32),
            pltpu.VMEM((heads * tq, 1), F32),
            pltpu.VMEM((heads * tq, R), F32),
        ],
    )
    return pl.pallas_call(
        functools.partial(_attn_kernel, heads=heads, tq=tq, scale=scale),
        grid_spec=grid_spec,
        out_shape=jax.ShapeDtypeStruct((T, out_w), act_dtype),
        compiler_params=_cparams(("parallel", "arbitrary")),
        name="attn_prompt",
    )(qi, kj, qabs, qpe, kcat, wuv)


def _attn_sample_kernel(pt_ref, qabs_ref, qpe_ref, knew_ref, wuv_ref, *rest,
                        heads, rows, valid, npages, scale):
    ckv_refs = rest[:npages]
    kpe_refs = rest[npages:2 * npages]
    o_ref = rest[2 * npages]
    qa_scr, qp_scr, m_scr, l_scr, acc_scr = rest[2 * npages + 1:]
    s = pl.program_id(1)
    ns = pl.num_programs(1)
    nkv = qabs_ref.shape[-1]

    @pl.when(s == 0)
    def _():
        qpe = qpe_ref[...]
        for h in range(heads):
            qa_scr[h * rows:(h + 1) * rows, :] = qabs_ref[h]
            qp_scr[h * rows:(h + 1) * rows, :] = qpe[:, h * ROPE_DIM:(h + 1) * ROPE_DIM]
        m_scr[...] = jnp.full(m_scr.shape, -jnp.inf, F32)
        l_scr[...] = jnp.zeros(l_scr.shape, F32)
        acc_scr[...] = jnp.zeros(acc_scr.shape, F32)

    qa = qa_scr[...].astype(BF16)
    qp = qp_scr[...].astype(BF16)

    def update(scores, values):
        m_old = m_scr[...]
        m_new = m_old
        for sc in scores:
            m_new = jnp.maximum(m_new, jnp.max(sc, axis=-1, keepdims=True))
        alpha = jnp.exp(m_old - m_new)
        l_new = alpha * l_scr[...]
        acc = alpha * acc_scr[...]
        for sc, v in zip(scores, values):
            p = jnp.exp(sc - m_new)
            l_new = l_new + jnp.sum(p, axis=-1, keepdims=True)
            acc = acc + _dot(p.astype(BF16), v)
        l_scr[...] = l_new
        acc_scr[...] = acc
        m_scr[...] = m_new

    scores, values = [], []
    for c_ref, p_ref in zip(ckv_refs, kpe_refs):
        kb = c_ref[...].astype(BF16)
        pb = p_ref[...].astype(BF16)
        scores.append((_dot_nt(qa, kb) + _dot_nt(qp, pb)) * scale)
        values.append(kb)
    update(scores, values)

    @pl.when(s == ns - 1)
    def _():
        kn = knew_ref[...]
        kn = jnp.concatenate([kn, jnp.zeros((LANES - rows, kn.shape[1]), kn.dtype)], axis=0).astype(BF16)
        kb = kn[:, :nkv]
        pb = kn[:, nkv:nkv + ROPE_DIM]
        sc = (_dot_nt(qa, kb) + _dot_nt(qp, pb)) * scale
        r = lax.broadcasted_iota(jnp.int32, sc.shape, 0) & (rows - 1)
        c = lax.broadcasted_iota(jnp.int32, sc.shape, 1)
        update([jnp.where((c <= r) & (c < valid), sc, -jnp.inf)], [kb])
        ob = (acc_scr[...] / l_scr[...]).astype(BF16)
        for p in range(heads // 2):
            lo = _dot(ob, wuv_ref[2 * p])[(2 * p) * rows:(2 * p + 1) * rows]
            hi = _dot(ob, wuv_ref[2 * p + 1])[(2 * p + 1) * rows:(2 * p + 2) * rows]
            o_ref[:, p * LANES:(p + 1) * LANES] = (lo + hi).astype(o_ref.dtype)


def _attn_sample(qabs, qpe, knew, wuv, cache_ckv, cache_kpe, page_table, layer, rows, valid, scale):
    heads, T, R = qabs.shape
    B, n_pages = page_table.shape
    npg = PAGES_PER_STEP
    steps = n_pages // npg
    _, _, page, klat = cache_ckv.shape
    krope = cache_kpe.shape[-1]
    out_w = (heads // 2) * LANES

    def page_spec(width, p):
        return pl.BlockSpec((None, None, page, width),
                            lambda b, s, pt: (layer, pt[b, s * npg + p], 0, 0))

    grid_spec = pltpu.PrefetchScalarGridSpec(
        num_scalar_prefetch=1,
        grid=(B, steps),
        in_specs=[
            pl.BlockSpec((heads, rows, R), lambda b, s, pt: (0, b, 0)),
            pl.BlockSpec((rows, qpe.shape[1]), lambda b, s, pt: (b, 0)),
            pl.BlockSpec((rows, knew.shape[1]), lambda b, s, pt: (b, 0)),
            pl.BlockSpec(wuv.shape, lambda b, s, pt: (0, 0, 0)),
        ] + [page_spec(klat, p) for p in range(npg)] + [page_spec(krope, p) for p in range(npg)],
        out_specs=pl.BlockSpec((rows, out_w), lambda b, s, pt: (b, 0)),
        scratch_shapes=[
            pltpu.VMEM((heads * rows, R), F32),
            pltpu.VMEM((heads * rows, ROPE_DIM), F32),
            pltpu.VMEM((heads * rows, 1), F32),
            pltpu.VMEM((heads * rows, 1), F32),
            pltpu.VMEM((heads * rows, R), F32),
        ],
    )
    return pl.pallas_call(
        functools.partial(_attn_sample_kernel, heads=heads, rows=rows, valid=valid, npages=npg, scale=scale),
        grid_spec=grid_spec,
        out_shape=jax.ShapeDtypeStruct((T, out_w), F32),
        compiler_params=_cparams(("parallel", "arbitrary")),
        name="attn_sample",
    )(page_table, qabs, qpe, knew, wuv, *([cache_ckv] * npg), *([cache_kpe] * npg))


def _ssd_kernel(xbc_ref, z_ref, dt_ref, dtT_ref, hist_ref, h0_ref, cw_ref, cb_ref,
                dtb_row_ref, dtb_col_ref, alog_row_ref, alog_col_ref, dskip_ref, gy_ref,
                y_ref, hlast_ref, xp_scr, h_scr, *, l, valid, heads, hdim, nstate, kconv):
    c = pl.program_id(1)
    nc = pl.num_programs(1)
    inner = heads * hdim
    gw = SSM_GROUPS * nstate
    hpg = heads // SSM_GROUPS
    pad = SUBLANES
    lb = xbc_ref.shape[0]

    def pad_rows(a):
        if lb == l:
            return a
        return jnp.concatenate([a, jnp.zeros((l - lb, a.shape[1]), a.dtype)], axis=0)

    @pl.when(c == 0)
    def _():
        xp_scr[0:pad, :] = hist_ref[...]
        h_scr[...] = h0_ref[...]

    xp_scr[pad:pad + l, :] = pad_rows(xbc_ref[...])
    acc = jnp.zeros((l, xbc_ref.shape[1]), F32) + cb_ref[...]
    for k in range(kconv):
        acc = acc + cw_ref[k:k + 1, :] * xp_scr[pl.ds(pad - (kconv - 1) + k, l), :]
    carry = xp_scr[l:l + pad, :]
    xp_scr[0:pad, :] = carry
    xc = _silu(acc)
    xs = xc[:, :inner]
    bm = xc[:, inner:inner + gw]
    cm = xc[:, inner + gw:inner + 2 * gw]

    dt_col = _softplus(pad_rows(dt_ref[...]) + dtb_row_ref[...])
    dt_row = _softplus(dtT_ref[...] + dtb_col_ref[...])
    if valid < l:
        dt_col = jnp.where(lax.broadcasted_iota(jnp.int32, dt_col.shape, 0) < valid, dt_col, 0.0)
        dt_row = jnp.where(lax.broadcasted_iota(jnp.int32, dt_row.shape, 1) < valid, dt_row, 0.0)
    dta_col = dt_col * (-jnp.exp(alog_row_ref[...]))
    dta_row = dt_row * (-jnp.exp(alog_col_ref[...]))

    ri = lax.broadcasted_iota(jnp.int32, (l, l), 0)
    ci = lax.broadcasted_iota(jnp.int32, (l, l), 1)
    causal = ci <= ri
    lower = jnp.where(causal, 1.0, 0.0).astype(BF16)
    upper = jnp.where(ri <= ci, 1.0, 0.0).astype(BF16)
    acs_col = sum(_dot(lower, piece) for piece in _split3(dta_col))
    acs_row = sum(_dot(piece, upper) for piece in _split3(dta_row))
    acs_last = acs_col[l - 1:l, :]
    eacs = jnp.exp(acs_col)
    toend = jnp.exp(acs_last - acs_col)

    lane = lax.broadcasted_iota(jnp.int32, (l, LANES), 1)
    per = LANES // hdim
    srow = lax.broadcasted_iota(jnp.int32, (LANES, nstate), 0)
    ys = []
    for p in range(heads // per):
        g = (p * per) // hpg
        bg = bm[:, g * nstate:(g + 1) * nstate].astype(BF16)
        cg = cm[:, g * nstate:(g + 1) * nstate]
        cbg = _dot_nt(cg.astype(BF16), bg)
        xs_p = xs[:, p * LANES:(p + 1) * LANES]
        dsk_p = dskip_ref[:, p * LANES:(p + 1) * LANES]
        hprev = h_scr[p * LANES:(p + 1) * LANES, :]
        hprev_b = hprev.astype(BF16)
        dt_p = jnp.zeros((l, LANES), F32)
        te_p = jnp.zeros((l, LANES), F32)
        cd_p = jnp.zeros((LANES, nstate), F32)
        for s in range(per):
            h = p * per + s
            sel = (lane // hdim) == s
            dt_p = jnp.where(sel, dt_col[:, h:h + 1], dt_p)
            te_p = jnp.where(sel, toend[:, h:h + 1], te_p)
            cd_p = jnp.where((srow // hdim) == s, jnp.exp(acs_row[h:h + 1, l - 1:l]), cd_p)
        dtx_p = dt_p * xs_p
        y_p = dsk_p * xs_p
        for s in range(per):
            h = p * per + s
            sel = (lane // hdim) == s
            seg = acs_col[:, h:h + 1] - acs_row[h:h + 1, :]
            dec = jnp.exp(jnp.where(causal, seg, -jnp.inf))
            mh = (cbg * dec).astype(BF16)
            y_h = _dot(mh, jnp.where(sel, dtx_p, 0.0).astype(BF16))
            cs = (cg * eacs[:, h:h + 1]).astype(BF16)
            y_h = y_h + jnp.where(sel, _dot_nt(cs, hprev_b), 0.0)
            y_p = y_p + y_h
        st_p = _dot_tn((dtx_p * te_p).astype(BF16), bg)
        h_scr[p * LANES:(p + 1) * LANES, :] = cd_p * hprev + st_p
        ys.append(y_p)
    y = jnp.concatenate(ys, axis=-1)[:lb] * _silu(z_ref[...])
    gsz = inner // SSM_GROUPS
    outs = [_rms(y[:, g * gsz:(g + 1) * gsz]) for g in range(SSM_GROUPS)]
    y_ref[...] = (jnp.concatenate(outs, axis=-1) * gy_ref[...]).astype(y_ref.dtype)

    @pl.when(c == nc - 1)
    def _():
        hlast_ref[...] = h_scr[...]


def _ssd(xbc, z, dt, dtT, hist, h0, wts, B, S, lb, valid, act_dtype):
    T, cdim = xbc.shape
    nc = S // lb
    l = SSD_CHUNK
    assert lb == l or nc == 1
    heads, hdim, nstate = wts["heads"], wts["hdim"], wts["nstate"]
    inner = heads * hdim
    kconv = wts["conv_w"].shape[0]
    row = lambda b, c: (b * nc + c, 0)
    fixed = lambda b, c: (0, 0)
    small = [wts["conv_w"], wts["conv_b"], wts["dtb_row"], wts["dtb_col"], wts["alog_row"], wts["alog_col"],
             wts["dskip"], wts["gy"]]
    return pl.pallas_call(
        functools.partial(_ssd_kernel, l=l, valid=valid, heads=heads, hdim=hdim, nstate=nstate, kconv=kconv),
        grid=(B, nc),
        in_specs=[
            pl.BlockSpec((lb, cdim), row),
            pl.BlockSpec((lb, inner), row),
            pl.BlockSpec((lb, dt.shape[1]), row),
            pl.BlockSpec((None, dtT.shape[1], l), lambda b, c: (b, 0, c)),
            pl.BlockSpec((None, SUBLANES, cdim), lambda b, c: (b, 0, 0)),
            pl.BlockSpec((None, inner, nstate), lambda b, c: (b, 0, 0)),
        ] + [pl.BlockSpec(a.shape, fixed) for a in small],
        out_specs=(pl.BlockSpec((lb, inner), row),
                   pl.BlockSpec((None, inner, nstate), lambda b, c: (b, 0, 0))),
        out_shape=(jax.ShapeDtypeStruct((T, inner), act_dtype),
                   jax.ShapeDtypeStruct((B, inner, nstate), F32)),
        scratch_shapes=[pltpu.VMEM((l + 2 * SUBLANES, cdim), F32),
                        pltpu.VMEM((inner, nstate), F32)],
        compiler_params=_cparams(("parallel", "arbitrary")),
        name="ssd",
    )(xbc, z, dt, dtT, hist, h0, *small)


def _outproj_kernel(a1_ref, a2_ref, w1_ref, w2_ref, x_ref, g_ref, b_ref, o_ref, *, alpha):
    m = _dot(a1_ref[...].astype(BF16), w1_ref[...]) + _dot(a2_ref[...].astype(BF16), w2_ref[...])
    o_ref[...] = _layer_norm(alpha * x_ref[...] + m, g_ref[...], b_ref[...])


def _outproj(a1, a2, w1, w2, x, g, b, alpha, tm):
    T, D = x.shape
    row = lambda i: (i, 0)
    fixed = lambda i: (0, 0)
    return pl.pallas_call(
        functools.partial(_outproj_kernel, alpha=alpha),
        grid=(T // tm,),
        in_specs=[pl.BlockSpec((tm, a1.shape[1]), row), pl.BlockSpec((tm, a2.shape[1]), row),
                  pl.BlockSpec(w1.shape, fixed), pl.BlockSpec(w2.shape, fixed),
                  pl.BlockSpec((tm, D), row), pl.BlockSpec(g.shape, fixed), pl.BlockSpec(b.shape, fixed)],
        out_specs=pl.BlockSpec((tm, D), row),
        out_shape=jax.ShapeDtypeStruct((T, D), F32),
        compiler_params=_cparams(("parallel",)),
        name="outproj_ln",
    )(a1, a2, w1, w2, x, g, b)


def _glu_kernel(x_ref, wa_ref, wb_ref, ba_ref, bb_ref, v_ref, *, tn):
    xb = x_ref[...].astype(BF16)
    for c in range(v_ref.shape[1] // tn):
        sl = slice(c * tn, (c + 1) * tn)
        ua = _dot(xb, wa_ref[:, sl]) + ba_ref[:, sl]
        ub = _dot(xb, wb_ref[:, sl]) + bb_ref[:, sl]
        v_ref[:, sl] = ua * jax.nn.sigmoid(ub)


def _glu(x, wa, wb, ba, bb, tm):
    T, D = x.shape
    C = wa.shape[1]
    row = lambda i: (i, 0)
    fixed = lambda i: (0, 0)
    return pl.pallas_call(
        functools.partial(_glu_kernel, tn=2 * LANES),
        grid=(T // tm,),
        in_specs=[pl.BlockSpec((tm, D), row), pl.BlockSpec(wa.shape, fixed), pl.BlockSpec(wb.shape, fixed),
                  pl.BlockSpec(ba.shape, fixed), pl.BlockSpec(bb.shape, fixed)],
        out_specs=pl.BlockSpec((tm, C), row),
        out_shape=jax.ShapeDtypeStruct((T, C), F32),
        compiler_params=_cparams(("parallel",)),
        name="conf_glu",
    )(x, wa, wb, ba, bb)


def _conf_kernel(v_ref, hist_ref, dww_ref, dwb_ref, gn_ref, bn_ref, wo_ref, bo_ref, x_ref, g_ref, b_ref,
                 o_ref, xp_scr, conv_scr, *, tm, halo, ktaps, rblk, alpha):
    c = pl.program_id(1)
    nlb = v_ref.shape[1] // LANES

    @pl.when(c == 0)
    def _():
        for cb in range(nlb):
            xp_scr[cb, 0:halo, :] = hist_ref[:, cb * LANES:(cb + 1) * LANES]

    for cb in range(nlb):
        xp_scr[cb, halo:halo + tm, :] = v_ref[:, cb * LANES:(cb + 1) * LANES]
    first = halo - (ktaps - 1)

    def lane_block(cb, carry):
        for rb in range(tm // rblk):
            acc = jnp.zeros((rblk, LANES), F32) + dwb_ref[cb]
            for k in range(ktaps):
                acc = acc + dww_ref[cb, k:k + 1, :] * xp_scr[cb, pl.ds(rb * rblk + first + k, rblk), :]
            conv_scr[cb, rb * rblk:(rb + 1) * rblk, :] = acc
        tail = xp_scr[cb, tm:tm + halo, :]
        xp_scr[cb, 0:halo, :] = tail
        return carry

    lax.fori_loop(0, nlb, lane_block, 0)
    conv = jnp.concatenate([conv_scr[cb] for cb in range(nlb)], axis=-1)
    cn = _silu(_layer_norm(conv, gn_ref[...], bn_ref[...]))
    if tm < 2 * SUBLANES:
        cn = jnp.concatenate([cn, jnp.zeros_like(cn)], axis=0)
    m = _dot(cn.astype(BF16), wo_ref[...])[:tm] + bo_ref[...]
    o_ref[...] = _layer_norm(alpha * x_ref[...] + m, g_ref[...], b_ref[...])


def _conf(v, hist, dww, dwb, gn, bn, wo, bo, x, g, b, B, S, tm, alpha):
    T, C = v.shape
    D = x.shape[1]
    nc = S // tm
    halo = hist.shape[1]
    ktaps = dww.shape[0]
    nlb = C // LANES
    dww = jnp.transpose(dww.reshape(ktaps, nlb, LANES), (1, 0, 2))
    dwb = dwb.reshape(nlb, 1, LANES)
    row = lambda bb, c: (bb * nc + c, 0)
    small = [dww, dwb, gn, bn, wo, bo]
    zeros = lambda a: (lambda bb, c: (0,) * a.ndim)
    return pl.pallas_call(
        functools.partial(_conf_kernel, tm=tm, halo=halo, ktaps=ktaps, rblk=min(tm, 64), alpha=alpha),
        grid=(B, nc),
        in_specs=[pl.BlockSpec((tm, C), row), pl.BlockSpec((None, halo, C), lambda bb, c: (bb, 0, 0))]
        + [pl.BlockSpec(a.shape, zeros(a)) for a in small]
        + [pl.BlockSpec((tm, D), row), pl.BlockSpec(g.shape, zeros(g)), pl.BlockSpec(b.shape, zeros(b))],
        out_specs=pl.BlockSpec((tm, D), row),
        out_shape=jax.ShapeDtypeStruct((T, D), F32),
        scratch_shapes=[pltpu.VMEM((nlb, tm + halo, LANES), F32), pltpu.VMEM((nlb, tm, LANES), F32)],
        compiler_params=_cparams(("parallel", "arbitrary")),
        name="conf_conv",
    )(v, hist, *small, x, g, b)


def _router_kernel(x_ref, wh_ref, wl_ref, rb_ref, g_ref, *, n_exp):
    x = x_ref[...]
    xh = x.astype(BF16)
    xl = (x - xh.astype(F32)).astype(BF16)
    logits = _dot_nt(wh_ref[...], xh) + _dot_nt(wl_ref[...], xh) + _dot_nt(wh_ref[...], xl)
    probs = jax.nn.sigmoid(logits)
    sel = probs + rb_ref[...]
    per = n_exp // N_GROUPS
    s = [sel[e:e + 1, :] for e in range(n_exp)]
    top = []
    for e in range(n_exp):
        g0 = (e // per) * per
        rank = jnp.zeros(s[e].shape, F32)
        for j in range(g0, g0 + per):
            if j < e:
                rank = rank + jnp.where(s[j] >= s[e], 1.0, 0.0)
            elif j > e:
                rank = rank + jnp.where(s[j] > s[e], 1.0, 0.0)
        top.append(rank < TOP_K)
    grp = []
    for g in range(N_GROUPS):
        tot = jnp.zeros(s[0].shape, F32)
        for e in range(g * per, (g + 1) * per):
            tot = tot + jnp.where(top[e], s[e], 0.0)
        grp.append(tot)
    chosen = []
    for g in range(N_GROUPS):
        ok = None
        for j in range(N_GROUPS):
            if j == g:
                continue
            t = (grp[j] < grp[g]) if j < g else (grp[j] <= grp[g])
            ok = t if ok is None else (ok & t)
        chosen.append(ok)
    w = [jnp.where(chosen[e // per] & top[e], probs[e:e + 1, :], 0.0) for e in range(n_exp)]
    den = w[0]
    for e in range(1, n_exp):
        den = den + w[e]
    for e in range(n_exp):
        g_ref[e:e + 1, :] = w[e] / den


def _router(x, wh, wl, rb, tm):
    T, D = x.shape
    E = wh.shape[0]
    return pl.pallas_call(
        functools.partial(_router_kernel, n_exp=E),
        grid=(T // tm,),
        in_specs=[pl.BlockSpec((tm, D), lambda i: (i, 0)), pl.BlockSpec(wh.shape, lambda i: (0, 0)),
                  pl.BlockSpec(wl.shape, lambda i: (0, 0)), pl.BlockSpec(rb.shape, lambda i: (0, 0))],
        out_specs=pl.BlockSpec((E, tm), lambda i: (0, i)),
        out_shape=jax.ShapeDtypeStruct((E, T), F32),
        compiler_params=_cparams(("parallel",)),
        name="router",
    )(x, wh, wl, rb)


def _moe_kernel(x_ref, gates_ref, wg_ref, wu_ref, wd_ref, g_ref, b_ref, o_ref, xb_scr, acc_scr, *, alpha):
    e = pl.program_id(1)
    ne = pl.num_programs(1)

    @pl.when(e == 0)
    def _():
        xb_scr[...] = x_ref[...].astype(BF16)
        acc_scr[...] = jnp.zeros(acc_scr.shape, F32)

    xb = xb_scr[...]
    gates = gates_ref[...]
    lane = lax.broadcasted_iota(jnp.int32, gates.shape, 1)
    gate = jnp.sum(jnp.where(lane == e, gates, 0.0), axis=-1, keepdims=True)
    h = _silu(_dot(xb, wg_ref[...].astype(BF16))) * _dot(xb, wu_ref[...].astype(BF16))
    acc_scr[...] += _dot((h * gate).astype(BF16), wd_ref[...].astype(BF16))

    @pl.when(e == ne - 1)
    def _():
        o_ref[...] = _layer_norm(alpha * x_ref[...] + acc_scr[...], g_ref[...], b_ref[...])


def _moe(x, gates, wg, wu, wd, g, b, layer, alpha, tm):
    T, D = x.shape
    _, E, _, F = wg.shape
    row = lambda i, e: (i, 0)
    fixed = lambda i, e: (0, 0)
    return pl.pallas_call(
        functools.partial(_moe_kernel, alpha=alpha),
        grid=(T // tm, E),
        in_specs=[pl.BlockSpec((tm, D), row), pl.BlockSpec((tm, E), row),
                  pl.BlockSpec((None, None, D, F), lambda i, e: (layer, e, 0, 0)),
                  pl.BlockSpec((None, None, D, F), lambda i, e: (layer, e, 0, 0)),
                  pl.BlockSpec((None, None, F, D), lambda i, e: (layer, e, 0, 0)),
                  pl.BlockSpec(g.shape, fixed), pl.BlockSpec(b.shape, fixed)],
        out_specs=pl.BlockSpec((tm, D), row),
        out_shape=jax.ShapeDtypeStruct((T, D), F32),
        scratch_shapes=[pltpu.VMEM((tm, D), BF16), pltpu.VMEM((tm, D), F32)],
        compiler_params=_cparams(("parallel", "arbitrary")),
        name="moe",
    )(x, gates, wg, wu, wd, g, b)


def _rope_tables(pos, heads, reps):
    half = ROPE_DIM // 2
    inv = ROPE_BASE ** (-jnp.arange(half, dtype=F32) / half)
    ang = pos.astype(F32)[:, None] * inv[None, :]
    cos, sin = jnp.cos(ang), jnp.sin(ang)
    cos_t = jnp.tile(jnp.concatenate([cos, cos], -1), (reps, heads))
    sin_t = jnp.tile(jnp.concatenate([-sin, sin], -1), (reps, heads))
    return cos_t, sin_t


def _swap_halves(w):
    half = ROPE_DIM // 2
    return jnp.concatenate([w[..., half:], w[..., :half]], axis=-1)


def _prep_ab(w_in, g_q, g_kv, w_uq, w_uk, w_uv, conv_w, conv_b, dt_bias, a_log, d_skip, g_y, w_out):
    q_lora = g_q.shape[0]
    kv_lora = g_kv.shape[0]
    heads = w_uq.shape[1]
    s_heads = dt_bias.shape[0]
    cdim = conv_w.shape[1]
    o1 = q_lora
    o2 = o1 + kv_lora
    o3 = o2 + ROPE_DIM
    inner = w_in.shape[1] - o3 - cdim - s_heads
    o4 = o3 + inner
    o5 = o4 + cdim
    hdim = inner // s_heads
    nstate = (cdim - inner) // (2 * SSM_GROUPS)
    wkp = w_in[:, o2:o3]
    dt_pad = jnp.pad(w_in[:, o5:], ((0, 0), (0, LANES - s_heads)))
    pieces = [("q", w_in[:, :o1]), ("kv", w_in[:, o1:o2]), ("kpa", jnp.tile(wkp, (1, heads))),
              ("kpb", jnp.tile(_swap_halves(wkp), (1, heads))), ("z", w_in[:, o3:o4]),
              ("xbc", w_in[:, o4:o5]), ("dt", dt_pad)]
    cols, off = {}, 0
    for name, p in pieces:
        cols[name] = (off, off + p.shape[1])
        off += p.shape[1]
    w_all = jnp.concatenate([p for _, p in pieces], axis=1).astype(BF16)

    wn = w_uq[:, :, :NOPE_DIM].reshape(q_lora, heads * NOPE_DIM).astype(BF16)
    wpe = w_uq[:, :, NOPE_DIM:]
    wpa = wpe.reshape(q_lora, heads * ROPE_DIM).astype(BF16)
    wpb = _swap_halves(wpe).reshape(q_lora, heads * ROPE_DIM).astype(BF16)
    per = LANES // NOPE_DIM
    ukt = jnp.transpose(w_uk, (1, 2, 0))
    wuk = jnp.stack([jnp.pad(ukt[h], (((h % per) * NOPE_DIM, (per - 1 - h % per) * NOPE_DIM), (0, 0)))
                     for h in range(heads)]).astype(BF16)
    v_dim = w_uv.shape[2]
    uvt = jnp.transpose(w_uv, (1, 0, 2))
    wuv = jnp.stack([jnp.pad(uvt[h], ((0, 0), ((h % 2) * v_dim, LANES - v_dim - (h % 2) * v_dim)))
                     for h in range(heads)]).astype(BF16)
    n_mla = heads * v_dim
    ssd = dict(
        heads=s_heads, hdim=hdim, nstate=nstate,
        conv_w=conv_w, conv_b=conv_b[None, :],
        dtb_row=jnp.pad(dt_bias, (0, LANES - s_heads))[None, :],
        dtb_col=jnp.pad(dt_bias, (0, 2 * SUBLANES - s_heads))[:, None],
        alog_row=jnp.pad(a_log, (0, LANES - s_heads))[None, :],
        alog_col=jnp.pad(a_log, (0, 2 * SUBLANES - s_heads))[:, None],
        dskip=jnp.repeat(d_skip, hdim)[None, :],
        gy=g_y[None, :],
    )
    return dict(cols=cols, w_all=w_all, gq=g_q[None, :], gkv=g_kv[None, :], wn=wn, wpa=wpa, wpb=wpb,
                wuk=wuk, wuv=wuv, ssd=ssd, heads=heads,
                w_out_a=w_out[:n_mla].astype(BF16), w_out_b=w_out[n_mla:].astype(BF16))


def _mixer_ab(x, grp, wts, tabs, ln_g, ln_b, alpha, paged=None):
    B, S, tm, l, valid, act = grp["B"], grp["S"], grp["tm"], grp["l"], grp["valid"], grp["act"]
    cos_t, sin_t = tabs
    hq, ckv, kpe, kcat, z, xbc, dt = _inproj(x, wts["w_all"], wts["gq"], wts["gkv"], cos_t, sin_t,
                                            wts["cols"], tm, act)
    qabs, qpe = _qprep(hq, wts["wn"], wts["wpa"], wts["wpb"], wts["wuk"], cos_t, sin_t, tm, act)
    scale = (NOPE_DIM + ROPE_DIM) ** -0.5
    if paged is None:
        o_mla = _attn_prompt(qabs, qpe, kcat, wts["wuv"], B, S, grp["tq"], scale, act)
    else:
        cache_ckv, cache_kpe, page_table, idx = paged
        o_mla = _attn_sample(qabs, qpe, kcat, wts["wuv"], cache_ckv, cache_kpe, page_table, idx, S, valid, scale)
    s_heads = wts["ssd"]["heads"]
    dtT = jnp.swapaxes(dt[:, :s_heads].reshape(B, S, s_heads), 1, 2)
    dtT = jnp.pad(dtT, ((0, 0), (0, 2 * SUBLANES - s_heads), (0, max(SSD_CHUNK - S, 0))))
    y, h_last = _ssd(xbc, z, dt, dtT, grp["ssm_hist"], grp["ssm_h0"], wts["ssd"], B, S, l, valid, act)
    x_new = _outproj(o_mla, y, wts["w_out_a"], wts["w_out_b"], x, ln_g, ln_b, alpha, tm)
    return x_new, ckv, kpe, h_last, xbc


def _mixer_c(x, grp, wts, ln_g, ln_b, alpha):
    B, S, tm = grp["B"], grp["S"], grp["tm"]
    v = _glu(x, wts["wa"], wts["wb"], wts["ba"], wts["bb"], tm)
    x_new = _conf(v, grp["conf_hist"], wts["dww"], wts["dwb"], wts["gn"], wts["bn"], wts["wo"], wts["bo"],
                  x, ln_g, ln_b, B, S, min(tm, S), alpha)
    return x_new, v


def _moe_block(x, rt, wg, wu, wd, ln_g, ln_b, layer, alpha, tm_r, tm_m):
    gates_t = _router(x, rt["wh"], rt["wl"], rt["rb"], tm_r)
    return _moe(x, gates_t.T, wg, wu, wd, ln_g, ln_b, layer, alpha, tm_m)


def kernel(x_prompt, x_sample, cache_ckv, cache_kpe, state_ssm, state_ssm_conv, state_conf_conv, page_table,
           w_in_ab, g_q_norm, g_kv_norm, w_uq, w_uk, w_uv, ssm_conv_w, ssm_conv_b, ssm_dt_bias, ssm_a_log,
           ssm_d, ssm_norm_g, w_out_ab, conf_w_in, conf_b_in, conf_dw_w, conf_dw_b, conf_norm_g, conf_norm_b,
           conf_w_out, conf_b_out, ln_mix_g, ln_mix_b, ln_ffn_g, ln_ffn_b, w_router, router_bias,
           moe_w_gate, moe_w_up, moe_w_down):
    bp, S, D = x_prompt.shape
    bd, t_new, _ = x_sample.shape
    depth = ln_mix_g.shape[0]
    past_len = page_table.shape[1] * PAGE_SIZE
    alpha = (2 * depth) ** 0.25
    heads = w_uq.shape[2]
    n_a = w_in_ab.shape[0]
    n_c = conf_w_in.shape[0]
    conf_ch = conf_dw_w.shape[2]
    conf_k = conf_dw_w.shape[1]
    kconv = ssm_conv_w.shape[1]
    cdim = ssm_conv_w.shape[2]
    rows = SAMPLE_ROWS
    tm_p = 256

    ab = [_prep_ab(w_in_ab[i], g_q_norm[i], g_kv_norm[i], w_uq[i], w_uk[i], w_uv[i], ssm_conv_w[i],
                   ssm_conv_b[i], ssm_dt_bias[i], ssm_a_log[i], ssm_d[i], ssm_norm_g[i], w_out_ab[i])
          for i in range(n_a)]
    cw = [dict(wa=conf_w_in[i][:, :conf_ch].astype(BF16), wb=conf_w_in[i][:, conf_ch:].astype(BF16),
               ba=conf_b_in[i][None, :conf_ch], bb=conf_b_in[i][None, conf_ch:],
               dww=conf_dw_w[i], dwb=conf_dw_b[i][None, :], gn=conf_norm_g[i][None, :],
               bn=conf_norm_b[i][None, :], wo=conf_w_out[i].astype(BF16), bo=conf_b_out[i][None, :])
          for i in range(n_c)]
    wr_t = w_router.T
    wr_h = wr_t.astype(BF16)
    rt = dict(wh=wr_h, wl=(wr_t - wr_h.astype(F32)).astype(BF16), rb=router_bias[:, None])

    halo = 4 * SUBLANES
    sub_heads = ssm_dt_bias.shape[1]
    inner = state_ssm.shape[2] * state_ssm.shape[3]
    nstate = state_ssm.shape[4]
    gp = dict(B=bp, S=S, tm=tm_p, tq=256, l=min(SSD_CHUNK, S), valid=min(SSD_CHUNK, S), act=BF16,
              ssm_hist=jnp.zeros((bp, SUBLANES, cdim), F32), ssm_h0=jnp.zeros((bp, inner, nstate), F32),
              conf_hist=jnp.zeros((bp, halo, conf_ch), F32))
    gs = dict(B=bd, S=rows, tm=bd * rows, l=rows, valid=t_new, act=F32)
    tabs_p = _rope_tables(jnp.arange(S, dtype=jnp.int32), heads, 1)
    pos_s = past_len + jnp.arange(rows, dtype=jnp.int32)
    tabs_s = _rope_tables(pos_s, heads, bd)

    hp = x_prompt.reshape(bp * S, D)
    hs = jnp.pad(x_sample, ((0, 0), (0, rows - t_new), (0, 0))).reshape(bd * rows, D)
    out_p = dict(ckv=[], kpe=[], ssm=[], sconv=[], cconv=[])
    out_s = dict(ckv=[], kpe=[], ssm=[], sconv=[], cconv=[])
    for layer in range(depth):
        i = layer // 2
        lg, lb = ln_mix_g[layer][None, :], ln_mix_b[layer][None, :]
        if layer % 2 == 0:
            hp, c1, k1, h1, xbc1 = _mixer_ab(hp, gp, ab[i], tabs_p, lg, lb, alpha)
            gs_l = dict(gs,
                        ssm_hist=jnp.pad(state_ssm_conv[i], ((0, 0), (SUBLANES - (kconv - 1), 0), (0, 0))),
                        ssm_h0=state_ssm[i].reshape(bd, inner, nstate))
            hs, c2, k2, h2, xbc2 = _mixer_ab(hs, gs_l, ab[i], tabs_s, lg, lb, alpha,
                                             paged=(cache_ckv, cache_kpe, page_table, i))
            out_p["ckv"].append(c1.reshape(bp, S, -1))
            out_p["kpe"].append(k1.reshape(bp, S, -1)[..., :ROPE_DIM])
            out_p["ssm"].append(h1.reshape(bp, sub_heads, -1, nstate))
            out_p["sconv"].append(xbc1.reshape(bp, S, cdim)[:, S - (kconv - 1):])
            out_s["ckv"].append(c2.reshape(bd, rows, -1)[:, :t_new])
            out_s["kpe"].append(k2.reshape(bd, rows, -1)[:, :t_new, :ROPE_DIM])
            out_s["ssm"].append(h2.reshape(bd, sub_heads, -1, nstate))
            xp = jnp.concatenate([state_ssm_conv[i], xbc2.reshape(bd, rows, cdim)[:, :t_new]], axis=1)
            out_s["sconv"].append(xp[:, xp.shape[1] - (kconv - 1):])
        else:
            hp, v1 = _mixer_c(hp, gp, cw[i], lg, lb, alpha)
            gs_l = dict(gs, conf_hist=jnp.pad(state_conf_conv[i], ((0, 0), (halo - (conf_k - 1), 0), (0, 0))))
            hs, v2 = _mixer_c(hs, gs_l, cw[i], lg, lb, alpha)
            out_p["cconv"].append(v1.reshape(bp, S, conf_ch)[:, S - (conf_k - 1):])
            vp = jnp.concatenate([state_conf_conv[i], v2.reshape(bd, rows, conf_ch)[:, :t_new]], axis=1)
            out_s["cconv"].append(vp[:, vp.shape[1] - (conf_k - 1):])
        fg, fb = ln_ffn_g[layer][None, :], ln_ffn_b[layer][None, :]
        hp = _moe_block(hp, rt, moe_w_gate, moe_w_up, moe_w_down, fg, fb, layer, alpha, 512, 1024)
        hs = _moe_block(hs, rt, moe_w_gate, moe_w_up, moe_w_down, fg, fb, layer, alpha, bd * rows, bd * rows)
    y_p = hp.reshape(bp, S, D)
    y_s = hs.reshape(bd, rows, D)[:, :t_new]
    return (y_p, y_s,
            jnp.stack(out_p["ckv"]), jnp.stack(out_p["kpe"]), jnp.stack(out_p["ssm"]),
            jnp.stack(out_p["sconv"]), jnp.stack(out_p["cconv"]),
            jnp.stack(out_s["ckv"]), jnp.stack(out_s["kpe"]), jnp.stack(out_s["ssm"]),
            jnp.stack(out_s["sconv"]), jnp.stack(out_s["cconv"]))
```

```python
import functools
import math

import jax
import jax.numpy as jnp
from jax import lax
from jax.experimental import pallas as pl
from jax.experimental.pallas import tpu as pltpu

F32 = jnp.float32
BF16 = jnp.bfloat16

PAGE_SIZE = 128
NOPE_DIM = 64
ROPE_DIM = 32
ROPE_BASE = 10000.0
SSM_GROUPS = 2
SSD_CHUNK = 128
N_GROUPS = 4
TOP_K = 2
EPS = 1e-6

LANES = 128
SUBLANES = 8
VMEM_LIMIT_BYTES = 56 * 1024 * 1024
SAMPLE_ROWS = SUBLANES
PAGES_PER_STEP = 16


def _dot(a, b):
    return jnp.dot(a, b, preferred_element_type=F32)


def _dot_nt(a, b):
    return lax.dot_general(a, b, (((1,), (1,)), ((), ())), preferred_element_type=F32)


def _dot_tn(a, b):
    return lax.dot_general(a, b, (((0,), (0,)), ((), ())), preferred_element_type=F32)


def _rms(x):
    return x * lax.rsqrt(jnp.mean(x * x, axis=-1, keepdims=True) + EPS)


def _layer_norm(r, g, b):
    mu = jnp.mean(r, axis=-1, keepdims=True)
    d = r - mu
    var = jnp.mean(d * d, axis=-1, keepdims=True)
    return d * lax.rsqrt(var + EPS) * g + b


def _silu(x):
    return x * jax.nn.sigmoid(x)


def _softplus(x):
    return jnp.maximum(x, 0.0) + jnp.log1p(jnp.exp(-jnp.abs(x)))


def _split3(x):
    hi = x.astype(BF16)
    r1 = x - hi.astype(F32)
    mid = r1.astype(BF16)
    lo = (r1 - mid.astype(F32)).astype(BF16)
    return hi, mid, lo


def _cparams(sem):
    return pltpu.CompilerParams(dimension_semantics=sem, vmem_limit_bytes=VMEM_LIMIT_BYTES)


def _inproj_kernel(x_ref, w_ref, gq_ref, gkv_ref, cos_ref, sin_ref,
                   hq_ref, ckv_ref, kpe_ref, kcat_ref, z_ref, xbc_ref, dt_ref, *, cols):
    xb = x_ref[...].astype(BF16)

    def proj(name):
        a, b = cols[name]
        return _dot(xb, w_ref[:, a:b])

    hq_ref[...] = (_rms(proj("q")) * gq_ref[...]).astype(hq_ref.dtype)
    ckv = _rms(proj("kv")) * gkv_ref[...]
    ckv_ref[...] = ckv
    kpe = proj("kpa") * cos_ref[...] + proj("kpb") * sin_ref[...]
    kpe_ref[...] = kpe
    nkv = ckv.shape[-1]
    kcat_ref[:, :nkv] = ckv.astype(kcat_ref.dtype)
    kcat_ref[:, nkv:] = kpe.astype(kcat_ref.dtype)
    z_ref[...] = proj("z")
    xbc_ref[...] = proj("xbc")
    dt_ref[...] = proj("dt")


def _inproj(x, w, gq, gkv, cos_t, sin_t, cols, tm, act_dtype):
    T, D = x.shape
    n_tab = cos_t.shape[0] // tm
    widths = {k: b - a for k, (a, b) in cols.items()}
    row = lambda i: (i, 0)
    fixed = lambda i: (0, 0)
    tab = lambda i: (i % n_tab, 0)
    out_shapes = (
        jax.ShapeDtypeStruct((T, widths["q"]), act_dtype),
        jax.ShapeDtypeStruct((T, widths["kv"]), F32),
        jax.ShapeDtypeStruct((T, widths["kpa"]), F32),
        jax.ShapeDtypeStruct((T, widths["kv"] + widths["kpa"]), act_dtype),
        jax.ShapeDtypeStruct((T, widths["z"]), F32),
        jax.ShapeDtypeStruct((T, widths["xbc"]), F32),
        jax.ShapeDtypeStruct((T, widths["dt"]), F32),
    )
    return pl.pallas_call(
        functools.partial(_inproj_kernel, cols=cols),
        grid=(T // tm,),
        in_specs=[
            pl.BlockSpec((tm, D), row),
            pl.BlockSpec(w.shape, fixed),
            pl.BlockSpec(gq.shape, fixed),
            pl.BlockSpec(gkv.shape, fixed),
            pl.BlockSpec((tm, cos_t.shape[1]), tab),
            pl.BlockSpec((tm, sin_t.shape[1]), tab),
        ],
        out_specs=tuple(pl.BlockSpec((tm, s.shape[1]), row) for s in out_shapes),
        out_shape=out_shapes,
        compiler_params=_cparams(("parallel",)),
        name="inproj",
    )(x, w, gq, gkv, cos_t, sin_t)


def _qprep_kernel(hq_ref, wn_ref, wpa_ref, wpb_ref, wuk_ref, cos_ref, sin_ref, qabs_ref, qpe_ref, *, heads):
    hq = hq_ref[...].astype(BF16)
    qn = _dot(hq, wn_ref[...])
    qpe = _dot(hq, wpa_ref[...]) * cos_ref[...] + _dot(hq, wpb_ref[...]) * sin_ref[...]
    qpe_ref[...] = qpe.astype(qpe_ref.dtype)
    per = LANES // NOPE_DIM
    for h in range(heads):
        slab = qn[:, (h // per) * LANES:(h // per + 1) * LANES].astype(BF16)
        qabs_ref[h] = _dot(slab, wuk_ref[h]).astype(qabs_ref.dtype)


def _qprep(hq, wn, wpa, wpb, wuk, cos_t, sin_t, tm, act_dtype):
    T, Q = hq.shape
    heads, _, R = wuk.shape
    n_tab = cos_t.shape[0] // tm
    row = lambda i: (i, 0)
    fixed2 = lambda i: (0, 0)
    tab = lambda i: (i % n_tab, 0)
    return pl.pallas_call(
        functools.partial(_qprep_kernel, heads=heads),
        grid=(T // tm,),
        in_specs=[
            pl.BlockSpec((tm, Q), row),
            pl.BlockSpec(wn.shape, fixed2),
            pl.BlockSpec(wpa.shape, fixed2),
            pl.BlockSpec(wpb.shape, fixed2),
            pl.BlockSpec(wuk.shape, lambda i: (0, 0, 0)),
            pl.BlockSpec((tm, cos_t.shape[1]), tab),
            pl.BlockSpec((tm, sin_t.shape[1]), tab),
        ],
        out_specs=(pl.BlockSpec((heads, tm, R), lambda i: (0, i, 0)),
                   pl.BlockSpec((tm, wpa.shape[1]), row)),
        out_shape=(jax.ShapeDtypeStruct((heads, T, R), act_dtype),
                   jax.ShapeDtypeStruct((T, wpa.shape[1]), act_dtype)),
        compiler_params=_cparams(("parallel",)),
        name="qprep",
    )(hq, wn, wpa, wpb, wuk, cos_t, sin_t)


def _attn_kernel(qi_ref, kj_ref, qabs_ref, qpe_ref, k_ref, wuv_ref, o_ref,
                 q_scr, m_scr, l_scr, acc_scr, *, heads, tq, tk, scale):
    s = pl.program_id(1)
    i = qi_ref[s]
    j = kj_ref[s]
    nkv = qabs_ref.shape[-1]
    c2 = scale * math.log2(math.e)

    @pl.when(j == 0)
    def _():
        qpe = qpe_ref[...]
        lane = lax.broadcasted_iota(jnp.int32, qpe.shape, 1)
        for h in range(heads):
            q_scr[h * tq:(h + 1) * tq, :nkv] = qabs_ref[h].astype(BF16)
            q_scr[h * tq:(h + 1) * tq, nkv:] = jnp.where(lane // ROPE_DIM == h, qpe, 0).astype(BF16)
        m_scr[...] = jnp.full(m_scr.shape, -jnp.inf, F32)
        l_scr[...] = jnp.zeros(l_scr.shape, F32)
        acc_scr[...] = jnp.zeros(acc_scr.shape, F32)

    k = k_ref[...]

    def update(sc):
        m_old = m_scr[...]
        m_new = jnp.maximum(m_old, jnp.max(sc, axis=-1, keepdims=True))
        alpha = jnp.exp2((m_old - m_new) * c2)
        p = jnp.exp2((sc - jnp.tile(m_new, (1, tk // LANES))) * c2)
        psum = p[:, :LANES]
        for t in range(1, tk // LANES):
            psum = psum + p[:, t * LANES:(t + 1) * LANES]
        l_scr[...] = alpha * l_scr[...] + psum
        acc_scr[...] = jnp.tile(alpha, (1, nkv // LANES)) * acc_scr[...] + _dot(p.astype(BF16), k[:, :nkv])
        m_scr[...] = m_new

    diag = ((i + 1) * tq - 1) // tk

    @pl.when(j < diag)
    def _():
        update(_dot_nt(q_scr[...], k))

    @pl.when(j == diag)
    def _():
        sc = _dot_nt(q_scr[...], k)
        r = lax.broadcasted_iota(jnp.int32, sc.shape, 0)
        c = lax.broadcasted_iota(jnp.int32, sc.shape, 1)
        update(jnp.where(j * tk + c <= i * tq + (r & (tq - 1)), sc, -jnp.inf))
        o = acc_scr[...] / jnp.sum(l_scr[...], axis=-1, keepdims=True)
        for p in range(heads // 2):
            lo = o[(2 * p) * tq:(2 * p + 1) * tq].astype(BF16)
            hi = o[(2 * p + 1) * tq:(2 * p + 2) * tq].astype(BF16)
            o_ref[:, p * LANES:(p + 1) * LANES] = (
                _dot(lo, wuv_ref[2 * p]) + _dot(hi, wuv_ref[2 * p + 1])).astype(o_ref.dtype)


def _attn_prompt(qabs, qpe, kcat, wuv, B, S, tq, tk, scale, act_dtype):
    heads, T, R = qabs.shape
    assert tk % tq == 0 and S % tk == 0
    nq = S // tq
    nk = S // tk
    pairs = [(i, j) for i in range(nq) for j in range(((i + 1) * tq - 1) // tk + 1)]
    qi = jnp.asarray([p[0] for p in pairs], jnp.int32)
    kj = jnp.asarray([p[1] for p in pairs], jnp.int32)
    kw = kcat.shape[1]
    out_w = (heads // 2) * LANES
    grid_spec = pltpu.PrefetchScalarGridSpec(
        num_scalar_prefetch=2,
        grid=(B, len(pairs)),
        in_specs=[
            pl.BlockSpec((heads, tq, R), lambda b, s, qi, kj: (0, b * nq + qi[s], 0)),
            pl.BlockSpec((tq, qpe.shape[1]), lambda b, s, qi, kj: (b * nq + qi[s], 0)),
            pl.BlockSpec((tk, kw), lambda b, s, qi, kj: (b * nk + kj[s], 0)),
            pl.BlockSpec(wuv.shape, lambda b, s, qi, kj: (0, 0, 0)),
        ],
        out_specs=pl.BlockSpec((tq, out_w), lambda b, s, qi, kj: (b * nq + qi[s], 0)),
        scratch_shapes=[
            pltpu.VMEM((heads * tq, kw), BF16),
            pltpu.VMEM((heads * tq, LANES), F32),
            pltpu.VMEM((heads * tq, LANES), F32),
            pltpu.VMEM((heads * tq, R), F32),
        ],
    )
    return pl.pallas_call(
        functools.partial(_attn_kernel, heads=heads, tq=tq, tk=tk, scale=scale),
        grid_spec=grid_spec,
        out_shape=jax.ShapeDtypeStruct((T, out_w), act_dtype),
        compiler_params=_cparams(("parallel", "arbitrary")),
        name="attn_prompt",
    )(qi, kj, qabs, qpe, kcat, wuv)


def _attn_sample_kernel(pt_ref, qabs_ref, qpe_ref, knew_ref, wuv_ref, *rest,
                        heads, rows, valid, npages, scale):
    ckv_refs = rest[:npages]
    kpe_refs = rest[npages:2 * npages]
    o_ref = rest[2 * npages]
    qa_scr, qp_scr, m_scr, l_scr, acc_scr = rest[2 * npages + 1:]
    s = pl.program_id(1)
    ns = pl.num_programs(1)
    nkv = qabs_ref.shape[-1]
    c2 = scale * math.log2(math.e)

    @pl.when(s == 0)
    def _():
        qpe = qpe_ref[...]
        for h in range(heads):
            qa_scr[h * rows:(h + 1) * rows, :] = qabs_ref[h]
            qp_scr[h * rows:(h + 1) * rows, :] = qpe[:, h * ROPE_DIM:(h + 1) * ROPE_DIM]
        m_scr[...] = jnp.full(m_scr.shape, -jnp.inf, F32)
        l_scr[...] = jnp.zeros(l_scr.shape, F32)
        acc_scr[...] = jnp.zeros(acc_scr.shape, F32)

    qa = qa_scr[...].astype(BF16)
    qp = qp_scr[...].astype(BF16)

    def update(sc, v):
        m_old = m_scr[...]
        m_new = jnp.maximum(m_old, jnp.max(sc, axis=-1, keepdims=True))
        alpha = jnp.exp2((m_old - m_new) * c2)
        p = jnp.exp2((sc - jnp.tile(m_new, (1, sc.shape[1] // LANES))) * c2)
        psum = p[:, :LANES]
        for t in range(1, sc.shape[1] // LANES):
            psum = psum + p[:, t * LANES:(t + 1) * LANES]
        l_scr[...] = alpha * l_scr[...] + psum
        acc_scr[...] = jnp.tile(alpha, (1, nkv // LANES)) * acc_scr[...] + _dot(p.astype(BF16), v)
        m_scr[...] = m_new

    kb = jnp.concatenate([c_ref[...].astype(BF16) for c_ref in ckv_refs], axis=0)
    pt = jnp.concatenate([p_ref[...] for p_ref in kpe_refs], axis=1).astype(BF16)
    update(_dot_nt(qa, kb) + _dot(qp, pt), kb)

    @pl.when(s == ns - 1)
    def _():
        kn = knew_ref[...]
        kn = jnp.concatenate([kn, jnp.zeros((LANES - rows, kn.shape[1]), kn.dtype)], axis=0).astype(BF16)
        kb = kn[:, :nkv]
        sc = _dot_nt(qa, kb) + _dot_nt(qp, kn[:, nkv:nkv + ROPE_DIM])
        tok = lax.broadcasted_iota(jnp.int32, sc.shape, 0) & (rows - 1)
        key = lax.broadcasted_iota(jnp.int32, sc.shape, 1)
        update(jnp.where((key <= tok) & (key < valid), sc, -jnp.inf), kb)
        ob = (acc_scr[...] / jnp.sum(l_scr[...], axis=-1, keepdims=True)).astype(BF16)
        for p in range(heads // 2):
            lo = _dot(ob, wuv_ref[2 * p])[(2 * p) * rows:(2 * p + 1) * rows]
            hi = _dot(ob, wuv_ref[2 * p + 1])[(2 * p + 1) * rows:(2 * p + 2) * rows]
            o_ref[:, p * LANES:(p + 1) * LANES] = (lo + hi).astype(o_ref.dtype)


def _attn_sample(qabs, qpe, knew, wuv, cache_ckv, cache_kpe, page_table, layer, rows, valid, scale):
    heads, T, R = qabs.shape
    B, n_pages = page_table.shape
    npg = PAGES_PER_STEP
    steps = n_pages // npg
    _, _, page, klat = cache_ckv.shape
    out_w = (heads // 2) * LANES
    kpeT = jnp.swapaxes(cache_kpe, 2, 3)

    def page_spec(shape, p):
        return pl.BlockSpec((None, None) + shape, lambda b, s, pt: (layer, pt[b, s * npg + p], 0, 0))

    grid_spec = pltpu.PrefetchScalarGridSpec(
        num_scalar_prefetch=1,
        grid=(B, steps),
        in_specs=[
            pl.BlockSpec((heads, rows, R), lambda b, s, pt: (0, b, 0)),
            pl.BlockSpec((rows, qpe.shape[1]), lambda b, s, pt: (b, 0)),
            pl.BlockSpec((rows, knew.shape[1]), lambda b, s, pt: (b, 0)),
            pl.BlockSpec(wuv.shape, lambda b, s, pt: (0, 0, 0)),
        ] + [page_spec((page, klat), p) for p in range(npg)]
        + [page_spec((ROPE_DIM, page), p) for p in range(npg)],
        out_specs=pl.BlockSpec((rows, out_w), lambda b, s, pt: (b, 0)),
        scratch_shapes=[
            pltpu.VMEM((heads * rows, R), F32),
            pltpu.VMEM((heads * rows, ROPE_DIM), F32),
            pltpu.VMEM((heads * rows, LANES), F32),
            pltpu.VMEM((heads * rows, LANES), F32),
            pltpu.VMEM((heads * rows, R), F32),
        ],
    )
    return pl.pallas_call(
        functools.partial(_attn_sample_kernel, heads=heads, rows=rows, valid=valid, npages=npg, scale=scale),
        grid_spec=grid_spec,
        out_shape=jax.ShapeDtypeStruct((T, out_w), F32),
        compiler_params=_cparams(("parallel", "arbitrary")),
        name="attn_sample",
    )(page_table, qabs, qpe, knew, wuv, *([cache_ckv] * npg), *([kpeT] * npg))


def _ssd_kernel(xbc_ref, z_ref, dt_ref, dtT_ref, hist_ref, h0_ref, cw_ref, cb_ref,
                dtb_row_ref, dtb_col_ref, alog_row_ref, alog_col_ref, dskip_ref, gy_ref,
                y_ref, hlast_ref, xp_scr, h_scr, *, l, valid, heads, hdim, nstate, kconv):
    c = pl.program_id(1)
    nc = pl.num_programs(1)
    inner = heads * hdim
    gw = SSM_GROUPS * nstate
    hpg = heads // SSM_GROUPS
    pad = SUBLANES
    lb = xbc_ref.shape[0]

    def pad_rows(a):
        if lb == l:
            return a
        return jnp.concatenate([a, jnp.zeros((l - lb, a.shape[1]), a.dtype)], axis=0)

    @pl.when(c == 0)
    def _():
        xp_scr[0:pad, :] = hist_ref[...]
        h_scr[...] = h0_ref[...]

    xp_scr[pad:pad + l, :] = pad_rows(xbc_ref[...])
    acc = jnp.zeros((l, xbc_ref.shape[1]), F32) + cb_ref[...]
    for k in range(kconv):
        acc = acc + cw_ref[k:k + 1, :] * xp_scr[pl.ds(pad - (kconv - 1) + k, l), :]
    carry = xp_scr[l:l + pad, :]
    xp_scr[0:pad, :] = carry
    xc = _silu(acc)
    xs = xc[:, :inner]
    bm = xc[:, inner:inner + gw]
    cm = xc[:, inner + gw:inner + 2 * gw]

    dt_col = _softplus(pad_rows(dt_ref[...]) + dtb_row_ref[...])
    dt_row = _softplus(dtT_ref[...] + dtb_col_ref[...])
    if valid < l:
        dt_col = jnp.where(lax.broadcasted_iota(jnp.int32, dt_col.shape, 0) < valid, dt_col, 0.0)
        dt_row = jnp.where(lax.broadcasted_iota(jnp.int32, dt_row.shape, 1) < valid, dt_row, 0.0)
    dta_col = dt_col * (-jnp.exp(alog_row_ref[...]))
    dta_row = dt_row * (-jnp.exp(alog_col_ref[...]))

    ri = lax.broadcasted_iota(jnp.int32, (l, l), 0)
    ci = lax.broadcasted_iota(jnp.int32, (l, l), 1)
    causal = ci <= ri
    lower = jnp.where(causal, 1.0, 0.0).astype(BF16)
    upper = jnp.where(ri <= ci, 1.0, 0.0).astype(BF16)
    acs_col = sum(_dot(lower, piece) for piece in _split3(dta_col))
    acs_row = sum(_dot(piece, upper) for piece in _split3(dta_row))
    acs_last = acs_col[l - 1:l, :]
    eacs = jnp.exp(acs_col)
    toend = jnp.exp(acs_last - acs_col)

    lane = lax.broadcasted_iota(jnp.int32, (l, LANES), 1)
    per = LANES // hdim
    srow = lax.broadcasted_iota(jnp.int32, (LANES, nstate), 0)
    ys = []
    for p in range(heads // per):
        g = (p * per) // hpg
        bg = bm[:, g * nstate:(g + 1) * nstate].astype(BF16)
        cg = cm[:, g * nstate:(g + 1) * nstate]
        cbg = _dot_nt(cg.astype(BF16), bg)
        xs_p = xs[:, p * LANES:(p + 1) * LANES]
        dsk_p = dskip_ref[:, p * LANES:(p + 1) * LANES]
        hprev = h_scr[p * LANES:(p + 1) * LANES, :]
        hprev_b = hprev.astype(BF16)
        dt_p = jnp.zeros((l, LANES), F32)
        te_p = jnp.zeros((l, LANES), F32)
        cd_p = jnp.zeros((LANES, nstate), F32)
        for s in range(per):
            h = p * per + s
            sel = (lane // hdim) == s
            dt_p = jnp.where(sel, dt_col[:, h:h + 1], dt_p)
            te_p = jnp.where(sel, toend[:, h:h + 1], te_p)
            cd_p = jnp.where((srow // hdim) == s, jnp.exp(acs_row[h:h + 1, l - 1:l]), cd_p)
        dtx_p = dt_p * xs_p
        y_p = dsk_p * xs_p
        for s in range(per):
            h = p * per + s
            sel = (lane // hdim) == s
            seg = acs_col[:, h:h + 1] - acs_row[h:h + 1, :]
            dec = jnp.exp(jnp.where(causal, seg, -jnp.inf))
            mh = (cbg * dec).astype(BF16)
            y_h = _dot(mh, jnp.where(sel, dtx_p, 0.0).astype(BF16))
            cs = (cg * eacs[:, h:h + 1]).astype(BF16)
            y_h = y_h + jnp.where(sel, _dot_nt(cs, hprev_b), 0.0)
            y_p = y_p + y_h
        st_p = _dot_tn((dtx_p * te_p).astype(BF16), bg)
        h_scr[p * LANES:(p + 1) * LANES, :] = cd_p * hprev + st_p
        ys.append(y_p)
    y = jnp.concatenate(ys, axis=-1)[:lb] * _silu(z_ref[...])
    gsz = inner // SSM_GROUPS
    outs = [_rms(y[:, g * gsz:(g + 1) * gsz]) for g in range(SSM_GROUPS)]
    y_ref[...] = (jnp.concatenate(outs, axis=-1) * gy_ref[...]).astype(y_ref.dtype)

    @pl.when(c == nc - 1)
    def _():
        hlast_ref[...] = h_scr[...]


def _ssd(xbc, z, dt, dtT, hist, h0, wts, B, S, lb, valid, act_dtype):
    T, cdim = xbc.shape
    nc = S // lb
    l = SSD_CHUNK
    assert lb == l or nc == 1
    heads, hdim, nstate = wts["heads"], wts["hdim"], wts["nstate"]
    inner = heads * hdim
    kconv = wts["conv_w"].shape[0]
    row = lambda b, c: (b * nc + c, 0)
    fixed = lambda b, c: (0, 0)
    small = [wts["conv_w"], wts["conv_b"], wts["dtb_row"], wts["dtb_col"], wts["alog_row"], wts["alog_col"],
             wts["dskip"], wts["gy"]]
    return pl.pallas_call(
        functools.partial(_ssd_kernel, l=l, valid=valid, heads=heads, hdim=hdim, nstate=nstate, kconv=kconv),
        grid=(B, nc),
        in_specs=[
            pl.BlockSpec((lb, cdim), row),
            pl.BlockSpec((lb, inner), row),
            pl.BlockSpec((lb, dt.shape[1]), row),
            pl.BlockSpec((None, dtT.shape[1], l), lambda b, c: (b, 0, c)),
            pl.BlockSpec((None, SUBLANES, cdim), lambda b, c: (b, 0, 0)),
            pl.BlockSpec((None, inner, nstate), lambda b, c: (b, 0, 0)),
        ] + [pl.BlockSpec(a.shape, fixed) for a in small],
        out_specs=(pl.BlockSpec((lb, inner), row),
                   pl.BlockSpec((None, inner, nstate), lambda b, c: (b, 0, 0))),
        out_shape=(jax.ShapeDtypeStruct((T, inner), act_dtype),
                   jax.ShapeDtypeStruct((B, inner, nstate), F32)),
        scratch_shapes=[pltpu.VMEM((l + 2 * SUBLANES, cdim), F32),
                        pltpu.VMEM((inner, nstate), F32)],
        compiler_params=_cparams(("parallel", "arbitrary")),
        name="ssd",
    )(xbc, z, dt, dtT, hist, h0, *small)


def _outproj_kernel(a1_ref, a2_ref, w1_ref, w2_ref, x_ref, g_ref, b_ref, o_ref, *, alpha):
    m = _dot(a1_ref[...].astype(BF16), w1_ref[...]) + _dot(a2_ref[...].astype(BF16), w2_ref[...])
    o_ref[...] = _layer_norm(alpha * x_ref[...] + m, g_ref[...], b_ref[...])


def _outproj(a1, a2, w1, w2, x, g, b, alpha, tm):
    T, D = x.shape
    row = lambda i: (i, 0)
    fixed = lambda i: (0, 0)
    return pl.pallas_call(
        functools.partial(_outproj_kernel, alpha=alpha),
        grid=(T // tm,),
        in_specs=[pl.BlockSpec((tm, a1.shape[1]), row), pl.BlockSpec((tm, a2.shape[1]), row),
                  pl.BlockSpec(w1.shape, fixed), pl.BlockSpec(w2.shape, fixed),
                  pl.BlockSpec((tm, D), row), pl.BlockSpec(g.shape, fixed), pl.BlockSpec(b.shape, fixed)],
        out_specs=pl.BlockSpec((tm, D), row),
        out_shape=jax.ShapeDtypeStruct((T, D), F32),
        compiler_params=_cparams(("parallel",)),
        name="outproj_ln",
    )(a1, a2, w1, w2, x, g, b)


def _glu_kernel(x_ref, wa_ref, wb_ref, ba_ref, bb_ref, v_ref, *, tn):
    xb = x_ref[...].astype(BF16)
    for c in range(v_ref.shape[1] // tn):
        sl = slice(c * tn, (c + 1) * tn)
        ua = _dot(xb, wa_ref[:, sl]) + ba_ref[:, sl]
        ub = _dot(xb, wb_ref[:, sl]) + bb_ref[:, sl]
        v_ref[:, sl] = ua * jax.nn.sigmoid(ub)


def _glu(x, wa, wb, ba, bb, tm):
    T, D = x.shape
    C = wa.shape[1]
    row = lambda i: (i, 0)
    fixed = lambda i: (0, 0)
    return pl.pallas_call(
        functools.partial(_glu_kernel, tn=2 * LANES),
        grid=(T // tm,),
        in_specs=[pl.BlockSpec((tm, D), row), pl.BlockSpec(wa.shape, fixed), pl.BlockSpec(wb.shape, fixed),
                  pl.BlockSpec(ba.shape, fixed), pl.BlockSpec(bb.shape, fixed)],
        out_specs=pl.BlockSpec((tm, C), row),
        out_shape=jax.ShapeDtypeStruct((T, C), F32),
        compiler_params=_cparams(("parallel",)),
        name="conf_glu",
    )(x, wa, wb, ba, bb)


def _conf_kernel(v_ref, hist_ref, dww_ref, dwb_ref, gn_ref, bn_ref, wo_ref, bo_ref, x_ref, g_ref, b_ref,
                 o_ref, xp_scr, conv_scr, *, tm, halo, ktaps, rblk, alpha):
    c = pl.program_id(1)
    nlb = v_ref.shape[1] // LANES

    @pl.when(c == 0)
    def _():
        for cb in range(nlb):
            xp_scr[cb, 0:halo, :] = hist_ref[:, cb * LANES:(cb + 1) * LANES]

    for cb in range(nlb):
        xp_scr[cb, halo:halo + tm, :] = v_ref[:, cb * LANES:(cb + 1) * LANES]
    first = halo - (ktaps - 1)

    def lane_block(cb, carry):
        for rb in range(tm // rblk):
            acc = jnp.zeros((rblk, LANES), F32) + dwb_ref[cb]
            for k in range(ktaps):
                acc = acc + dww_ref[cb, k:k + 1, :] * xp_scr[cb, pl.ds(rb * rblk + first + k, rblk), :]
            conv_scr[cb, rb * rblk:(rb + 1) * rblk, :] = acc
        tail = xp_scr[cb, tm:tm + halo, :]
        xp_scr[cb, 0:halo, :] = tail
        return carry

    lax.fori_loop(0, nlb, lane_block, 0)
    conv = jnp.concatenate([conv_scr[cb] for cb in range(nlb)], axis=-1)
    cn = _silu(_layer_norm(conv, gn_ref[...], bn_ref[...]))
    if tm < 2 * SUBLANES:
        cn = jnp.concatenate([cn, jnp.zeros_like(cn)], axis=0)
    m = _dot(cn.astype(BF16), wo_ref[...])[:tm] + bo_ref[...]
    o_ref[...] = _layer_norm(alpha * x_ref[...] + m, g_ref[...], b_ref[...])


def _conf(v, hist, dww, dwb, gn, bn, wo, bo, x, g, b, B, S, tm, alpha):
    T, C = v.shape
    D = x.shape[1]
    nc = S // tm
    halo = hist.shape[1]
    ktaps = dww.shape[0]
    nlb = C // LANES
    dww = jnp.transpose(dww.reshape(ktaps, nlb, LANES), (1, 0, 2))
    dwb = dwb.reshape(nlb, 1, LANES)
    row = lambda bb, c: (bb * nc + c, 0)
    small = [dww, dwb, gn, bn, wo, bo]
    zeros = lambda a: (lambda bb, c: (0,) * a.ndim)
    return pl.pallas_call(
        functools.partial(_conf_kernel, tm=tm, halo=halo, ktaps=ktaps, rblk=min(tm, 64), alpha=alpha),
        grid=(B, nc),
        in_specs=[pl.BlockSpec((tm, C), row), pl.BlockSpec((None, halo, C), lambda bb, c: (bb, 0, 0))]
        + [pl.BlockSpec(a.shape, zeros(a)) for a in small]
        + [pl.BlockSpec((tm, D), row), pl.BlockSpec(g.shape, zeros(g)), pl.BlockSpec(b.shape, zeros(b))],
        out_specs=pl.BlockSpec((tm, D), row),
        out_shape=jax.ShapeDtypeStruct((T, D), F32),
        scratch_shapes=[pltpu.VMEM((nlb, tm + halo, LANES), F32), pltpu.VMEM((nlb, tm, LANES), F32)],
        compiler_params=_cparams(("parallel", "arbitrary")),
        name="conf_conv",
    )(v, hist, *small, x, g, b)


def _router_kernel(x_ref, wh_ref, wl_ref, rb_ref, g_ref, *, n_exp):
    x = x_ref[...]
    xh = x.astype(BF16)
    xl = (x - xh.astype(F32)).astype(BF16)
    logits = _dot_nt(wh_ref[...], xh) + _dot_nt(wl_ref[...], xh) + _dot_nt(wh_ref[...], xl)
    probs = jax.nn.sigmoid(logits)
    sel = probs + rb_ref[...]
    per = n_exp // N_GROUPS
    s = [sel[e:e + 1, :] for e in range(n_exp)]
    top = []
    for e in range(n_exp):
        g0 = (e // per) * per
        rank = jnp.zeros(s[e].shape, F32)
        for j in range(g0, g0 + per):
            if j < e:
                rank = rank + jnp.where(s[j] >= s[e], 1.0, 0.0)
            elif j > e:
                rank = rank + jnp.where(s[j] > s[e], 1.0, 0.0)
        top.append(rank < TOP_K)
    grp = []
    for g in range(N_GROUPS):
        tot = jnp.zeros(s[0].shape, F32)
        for e in range(g * per, (g + 1) * per):
            tot = tot + jnp.where(top[e], s[e], 0.0)
        grp.append(tot)
    chosen = []
    for g in range(N_GROUPS):
        ok = None
        for j in range(N_GROUPS):
            if j == g:
                continue
            t = (grp[j] < grp[g]) if j < g else (grp[j] <= grp[g])
            ok = t if ok is None else (ok & t)
        chosen.append(ok)
    w = [jnp.where(chosen[e // per] & top[e], probs[e:e + 1, :], 0.0) for e in range(n_exp)]
    den = w[0]
    for e in range(1, n_exp):
        den = den + w[e]
    for e in range(n_exp):
        g_ref[e:e + 1, :] = w[e] / den


def _router(x, wh, wl, rb, tm):
    T, D = x.shape
    E = wh.shape[0]
    return pl.pallas_call(
        functools.partial(_router_kernel, n_exp=E),
        grid=(T // tm,),
        in_specs=[pl.BlockSpec((tm, D), lambda i: (i, 0)), pl.BlockSpec(wh.shape, lambda i: (0, 0)),
                  pl.BlockSpec(wl.shape, lambda i: (0, 0)), pl.BlockSpec(rb.shape, lambda i: (0, 0))],
        out_specs=pl.BlockSpec((E, tm), lambda i: (0, i)),
        out_shape=jax.ShapeDtypeStruct((E, T), F32),
        compiler_params=_cparams(("parallel",)),
        name="router",
    )(x, wh, wl, rb)


def _moe_kernel(x_ref, gates_ref, wg_ref, wu_ref, wd_ref, g_ref, b_ref, o_ref, xb_scr, acc_scr, *, alpha):
    e = pl.program_id(1)
    ne = pl.num_programs(1)

    @pl.when(e == 0)
    def _():
        xb_scr[...] = x_ref[...].astype(BF16)
        acc_scr[...] = jnp.zeros(acc_scr.shape, F32)

    xb = xb_scr[...]
    gates = gates_ref[...]
    lane = lax.broadcasted_iota(jnp.int32, gates.shape, 1)
    gate = jnp.sum(jnp.where(lane == e, gates, 0.0), axis=-1, keepdims=True)
    h = _silu(_dot(xb, wg_ref[...].astype(BF16))) * _dot(xb, wu_ref[...].astype(BF16))
    acc_scr[...] += _dot((h * gate).astype(BF16), wd_ref[...].astype(BF16))

    @pl.when(e == ne - 1)
    def _():
        o_ref[...] = _layer_norm(alpha * x_ref[...] + acc_scr[...], g_ref[...], b_ref[...])


def _moe(x, gates, wg, wu, wd, g, b, layer, alpha, tm):
    T, D = x.shape
    _, E, _, F = wg.shape
    row = lambda i, e: (i, 0)
    fixed = lambda i, e: (0, 0)
    return pl.pallas_call(
        functools.partial(_moe_kernel, alpha=alpha),
        grid=(T // tm, E),
        in_specs=[pl.BlockSpec((tm, D), row), pl.BlockSpec((tm, E), row),
                  pl.BlockSpec((None, None, D, F), lambda i, e: (layer, e, 0, 0)),
                  pl.BlockSpec((None, None, D, F), lambda i, e: (layer, e, 0, 0)),
                  pl.BlockSpec((None, None, F, D), lambda i, e: (layer, e, 0, 0)),
                  pl.BlockSpec(g.shape, fixed), pl.BlockSpec(b.shape, fixed)],
        out_specs=pl.BlockSpec((tm, D), row),
        out_shape=jax.ShapeDtypeStruct((T, D), F32),
        scratch_shapes=[pltpu.VMEM((tm, D), BF16), pltpu.VMEM((tm, D), F32)],
        compiler_params=_cparams(("parallel", "arbitrary")),
        name="moe",
    )(x, gates, wg, wu, wd, g, b)


def _rope_tables(pos, heads, reps):
    half = ROPE_DIM // 2
    inv = ROPE_BASE ** (-jnp.arange(half, dtype=F32) / half)
    ang = pos.astype(F32)[:, None] * inv[None, :]
    cos, sin = jnp.cos(ang), jnp.sin(ang)
    cos_t = jnp.tile(jnp.concatenate([cos, cos], -1), (reps, heads))
    sin_t = jnp.tile(jnp.concatenate([-sin, sin], -1), (reps, heads))
    return cos_t, sin_t


def _swap_halves(w):
    half = ROPE_DIM // 2
    return jnp.concatenate([w[..., half:], w[..., :half]], axis=-1)


def _prep_ab(w_in, g_q, g_kv, w_uq, w_uk, w_uv, conv_w, conv_b, dt_bias, a_log, d_skip, g_y, w_out):
    q_lora = g_q.shape[0]
    kv_lora = g_kv.shape[0]
    heads = w_uq.shape[1]
    s_heads = dt_bias.shape[0]
    cdim = conv_w.shape[1]
    o1 = q_lora
    o2 = o1 + kv_lora
    o3 = o2 + ROPE_DIM
    inner = w_in.shape[1] - o3 - cdim - s_heads
    o4 = o3 + inner
    o5 = o4 + cdim
    hdim = inner // s_heads
    nstate = (cdim - inner) // (2 * SSM_GROUPS)
    wkp = w_in[:, o2:o3]
    dt_pad = jnp.pad(w_in[:, o5:], ((0, 0), (0, LANES - s_heads)))
    pieces = [("q", w_in[:, :o1]), ("kv", w_in[:, o1:o2]), ("kpa", jnp.tile(wkp, (1, heads))),
              ("kpb", jnp.tile(_swap_halves(wkp), (1, heads))), ("z", w_in[:, o3:o4]),
              ("xbc", w_in[:, o4:o5]), ("dt", dt_pad)]
    cols, off = {}, 0
    for name, p in pieces:
        cols[name] = (off, off + p.shape[1])
        off += p.shape[1]
    w_all = jnp.concatenate([p for _, p in pieces], axis=1).astype(BF16)

    wn = w_uq[:, :, :NOPE_DIM].reshape(q_lora, heads * NOPE_DIM).astype(BF16)
    wpe = w_uq[:, :, NOPE_DIM:]
    wpa = wpe.reshape(q_lora, heads * ROPE_DIM).astype(BF16)
    wpb = _swap_halves(wpe).reshape(q_lora, heads * ROPE_DIM).astype(BF16)
    per = LANES // NOPE_DIM
    ukt = jnp.transpose(w_uk, (1, 2, 0))
    wuk = jnp.stack([jnp.pad(ukt[h], (((h % per) * NOPE_DIM, (per - 1 - h % per) * NOPE_DIM), (0, 0)))
                     for h in range(heads)]).astype(BF16)
    v_dim = w_uv.shape[2]
    uvt = jnp.transpose(w_uv, (1, 0, 2))
    wuv = jnp.stack([jnp.pad(uvt[h], ((0, 0), ((h % 2) * v_dim, LANES - v_dim - (h % 2) * v_dim)))
                     for h in range(heads)]).astype(BF16)
    n_mla = heads * v_dim
    ssd = dict(
        heads=s_heads, hdim=hdim, nstate=nstate,
        conv_w=conv_w, conv_b=conv_b[None, :],
        dtb_row=jnp.pad(dt_bias, (0, LANES - s_heads))[None, :],
        dtb_col=jnp.pad(dt_bias, (0, 2 * SUBLANES - s_heads))[:, None],
        alog_row=jnp.pad(a_log, (0, LANES - s_heads))[None, :],
        alog_col=jnp.pad(a_log, (0, 2 * SUBLANES - s_heads))[:, None],
        dskip=jnp.repeat(d_skip, hdim)[None, :],
        gy=g_y[None, :],
    )
    return dict(cols=cols, w_all=w_all, gq=g_q[None, :], gkv=g_kv[None, :], wn=wn, wpa=wpa, wpb=wpb,
                wuk=wuk, wuv=wuv, ssd=ssd, heads=heads,
                w_out_a=w_out[:n_mla].astype(BF16), w_out_b=w_out[n_mla:].astype(BF16))


def _mixer_ab(x, grp, wts, tabs, ln_g, ln_b, alpha, paged=None):
    B, S, tm, l, valid, act = grp["B"], grp["S"], grp["tm"], grp["l"], grp["valid"], grp["act"]
    cos_t, sin_t = tabs
    hq, ckv, kpe, kcat, z, xbc, dt = _inproj(x, wts["w_all"], wts["gq"], wts["gkv"], cos_t, sin_t,
                                            wts["cols"], tm, act)
    qabs, qpe = _qprep(hq, wts["wn"], wts["wpa"], wts["wpb"], wts["wuk"], cos_t, sin_t, tm, act)
    scale = (NOPE_DIM + ROPE_DIM) ** -0.5
    if paged is None:
        o_mla = _attn_prompt(qabs, qpe, kcat, wts["wuv"], B, S, grp["tq"], grp["tk"], scale, act)
    else:
        cache_ckv, cache_kpe, page_table, idx = paged
        o_mla = _attn_sample(qabs, qpe, kcat, wts["wuv"], cache_ckv, cache_kpe, page_table, idx, S, valid, scale)
    s_heads = wts["ssd"]["heads"]
    dtT = jnp.swapaxes(dt[:, :s_heads].reshape(B, S, s_heads), 1, 2)
    dtT = jnp.pad(dtT, ((0, 0), (0, 2 * SUBLANES - s_heads), (0, max(SSD_CHUNK - S, 0))))
    y, h_last = _ssd(xbc, z, dt, dtT, grp["ssm_hist"], grp["ssm_h0"], wts["ssd"], B, S, l, valid, act)
    x_new = _outproj(o_mla, y, wts["w_out_a"], wts["w_out_b"], x, ln_g, ln_b, alpha, tm)
    return x_new, ckv, kpe, h_last, xbc


def _mixer_c(x, grp, wts, ln_g, ln_b, alpha):
    B, S, tm = grp["B"], grp["S"], grp["tm"]
    v = _glu(x, wts["wa"], wts["wb"], wts["ba"], wts["bb"], tm)
    x_new = _conf(v, grp["conf_hist"], wts["dww"], wts["dwb"], wts["gn"], wts["bn"], wts["wo"], wts["bo"],
                  x, ln_g, ln_b, B, S, min(tm, S), alpha)
    return x_new, v


def _moe_block(x, rt, wg, wu, wd, ln_g, ln_b, layer, alpha, tm_r, tm_m):
    gates_t = _router(x, rt["wh"], rt["wl"], rt["rb"], tm_r)
    return _moe(x, gates_t.T, wg, wu, wd, ln_g, ln_b, layer, alpha, tm_m)


def kernel(x_prompt, x_sample, cache_ckv, cache_kpe, state_ssm, state_ssm_conv, state_conf_conv, page_table,
           w_in_ab, g_q_norm, g_kv_norm, w_uq, w_uk, w_uv, ssm_conv_w, ssm_conv_b, ssm_dt_bias, ssm_a_log,
           ssm_d, ssm_norm_g, w_out_ab, conf_w_in, conf_b_in, conf_dw_w, conf_dw_b, conf_norm_g, conf_norm_b,
           conf_w_out, conf_b_out, ln_mix_g, ln_mix_b, ln_ffn_g, ln_ffn_b, w_router, router_bias,
           moe_w_gate, moe_w_up, moe_w_down):
    bp, S, D = x_prompt.shape
    bd, t_new, _ = x_sample.shape
    depth = ln_mix_g.shape[0]
    past_len = page_table.shape[1] * PAGE_SIZE
    alpha = (2 * depth) ** 0.25
    heads = w_uq.shape[2]
    n_a = w_in_ab.shape[0]
    n_c = conf_w_in.shape[0]
    conf_ch = conf_dw_w.shape[2]
    conf_k = conf_dw_w.shape[1]
    kconv = ssm_conv_w.shape[1]
    cdim = ssm_conv_w.shape[2]
    rows = SAMPLE_ROWS
    tm_p = 256

    ab = [_prep_ab(w_in_ab[i], g_q_norm[i], g_kv_norm[i], w_uq[i], w_uk[i], w_uv[i], ssm_conv_w[i],
                   ssm_conv_b[i], ssm_dt_bias[i], ssm_a_log[i], ssm_d[i], ssm_norm_g[i], w_out_ab[i])
          for i in range(n_a)]
    cw = [dict(wa=conf_w_in[i][:, :conf_ch].astype(BF16), wb=conf_w_in[i][:, conf_ch:].astype(BF16),
               ba=conf_b_in[i][None, :conf_ch], bb=conf_b_in[i][None, conf_ch:],
               dww=conf_dw_w[i], dwb=conf_dw_b[i][None, :], gn=conf_norm_g[i][None, :],
               bn=conf_norm_b[i][None, :], wo=conf_w_out[i].astype(BF16), bo=conf_b_out[i][None, :])
          for i in range(n_c)]
    wr_t = w_router.T
    wr_h = wr_t.astype(BF16)
    rt = dict(wh=wr_h, wl=(wr_t - wr_h.astype(F32)).astype(BF16), rb=router_bias[:, None])

    halo = 4 * SUBLANES
    sub_heads = ssm_dt_bias.shape[1]
    inner = state_ssm.shape[2] * state_ssm.shape[3]
    nstate = state_ssm.shape[4]
    gp = dict(B=bp, S=S, tm=tm_p, tq=256, tk=512, l=min(SSD_CHUNK, S), valid=min(SSD_CHUNK, S), act=BF16,
              ssm_hist=jnp.zeros((bp, SUBLANES, cdim), F32), ssm_h0=jnp.zeros((bp, inner, nstate), F32),
              conf_hist=jnp.zeros((bp, halo, conf_ch), F32))
    gs = dict(B=bd, S=rows, tm=bd * rows, l=rows, valid=t_new, act=F32)
    tabs_p = _rope_tables(jnp.arange(S, dtype=jnp.int32), heads, 1)
    pos_s = past_len + jnp.arange(rows, dtype=jnp.int32)
    tabs_s = _rope_tables(pos_s, heads, bd)

    hp = x_prompt.reshape(bp * S, D)
    hs = jnp.pad(x_sample, ((0, 0), (0, rows - t_new), (0, 0))).reshape(bd * rows, D)
    out_p = dict(ckv=[], kpe=[], ssm=[], sconv=[], cconv=[])
    out_s = dict(ckv=[], kpe=[], ssm=[], sconv=[], cconv=[])
    for layer in range(depth):
        i = layer // 2
        lg, lb = ln_mix_g[layer][None, :], ln_mix_b[layer][None, :]
        if layer % 2 == 0:
            hp, c1, k1, h1, xbc1 = _mixer_ab(hp, gp, ab[i], tabs_p, lg, lb, alpha)
            gs_l = dict(gs,
                        ssm_hist=jnp.pad(state_ssm_conv[i], ((0, 0), (SUBLANES - (kconv - 1), 0), (0, 0))),
                        ssm_h0=state_ssm[i].reshape(bd, inner, nstate))
            hs, c2, k2, h2, xbc2 = _mixer_ab(hs, gs_l, ab[i], tabs_s, lg, lb, alpha,
                                             paged=(cache_ckv, cache_kpe, page_table, i))
            out_p["ckv"].append(c1.reshape(bp, S, -1))
            out_p["kpe"].append(k1.reshape(bp, S, -1)[..., :ROPE_DIM])
            out_p["ssm"].append(h1.reshape(bp, sub_heads, -1, nstate))
            out_p["sconv"].append(xbc1.reshape(bp, S, cdim)[:, S - (kconv - 1):])
            out_s["ckv"].append(c2.reshape(bd, rows, -1)[:, :t_new])
            out_s["kpe"].append(k2.reshape(bd, rows, -1)[:, :t_new, :ROPE_DIM])
            out_s["ssm"].append(h2.reshape(bd, sub_heads, -1, nstate))
            xp = jnp.concatenate([state_ssm_conv[i], xbc2.reshape(bd, rows, cdim)[:, :t_new]], axis=1)
            out_s["sconv"].append(xp[:, xp.shape[1] - (kconv - 1):])
        else:
            hp, v1 = _mixer_c(hp, gp, cw[i], lg, lb, alpha)
            gs_l = dict(gs, conf_hist=jnp.pad(state_conf_conv[i], ((0, 0), (halo - (conf_k - 1), 0), (0, 0))))
            hs, v2 = _mixer_c(hs, gs_l, cw[i], lg, lb, alpha)
            out_p["cconv"].append(v1.reshape(bp, S, conf_ch)[:, S - (conf_k - 1):])
            vp = jnp.concatenate([state_conf_conv[i], v2.reshape(bd, rows, conf_ch)[:, :t_new]], axis=1)
            out_s["cconv"].append(vp[:, vp.shape[1] - (conf_k - 1):])
        fg, fb = ln_ffn_g[layer][None, :], ln_ffn_b[layer][None, :]
        hp = _moe_block(hp, rt, moe_w_gate, moe_w_up, moe_w_down, fg, fb, layer, alpha, 512, 1024)
        hs = _moe_block(hs, rt, moe_w_gate, moe_w_up, moe_w_down, fg, fb, layer, alpha, bd * rows, bd * rows)
    y_p = hp.reshape(bp, S, D)
    y_s = hs.reshape(bd, rows, D)[:, :t_new]
    return (y_p, y_s,
            jnp.stack(out_p["ckv"]), jnp.stack(out_p["kpe"]), jnp.stack(out_p["ssm"]),
            jnp.stack(out_p["sconv"]), jnp.stack(out_p["cconv"]),
            jnp.stack(out_s["ckv"]), jnp.stack(out_s["kpe"]), jnp.stack(out_s["ssm"]),
            jnp.stack(out_s["sconv"]), jnp.stack(out_s["cconv"]))
```

```python
import functools
import math

import jax
import jax.numpy as jnp
from jax import lax
from jax.experimental import pallas as pl
from jax.experimental.pallas import tpu as pltpu

F32 = jnp.float32
BF16 = jnp.bfloat16

PAGE_SIZE = 128
NOPE_DIM = 64
ROPE_DIM = 32
ROPE_BASE = 10000.0
SSM_GROUPS = 2
SSD_CHUNK = 128
N_GROUPS = 4
TOP_K = 2
EPS = 1e-6

LANES = 128
SUBLANES = 8
VMEM_LIMIT_BYTES = 56 * 1024 * 1024
SAMPLE_ROWS = SUBLANES
PAGES_PER_STEP = 16


def _dot(a, b):
    return jnp.dot(a, b, preferred_element_type=F32)


def _dot_nt(a, b):
    return lax.dot_general(a, b, (((1,), (1,)), ((), ())), preferred_element_type=F32)


def _dot_tn(a, b):
    return lax.dot_general(a, b, (((0,), (0,)), ((), ())), preferred_element_type=F32)


def _rms(x):
    return x * lax.rsqrt(jnp.mean(x * x, axis=-1, keepdims=True) + EPS)


def _layer_norm(r, g, b):
    mu = jnp.mean(r, axis=-1, keepdims=True)
    d = r - mu
    var = jnp.mean(d * d, axis=-1, keepdims=True)
    return d * lax.rsqrt(var + EPS) * g + b


def _silu(x):
    return x * jax.nn.sigmoid(x)


def _softplus(x):
    return jnp.maximum(x, 0.0) + jnp.log1p(jnp.exp(-jnp.abs(x)))


def _split3(x):
    hi = x.astype(BF16)
    r1 = x - hi.astype(F32)
    mid = r1.astype(BF16)
    lo = (r1 - mid.astype(F32)).astype(BF16)
    return hi, mid, lo


def _cparams(sem):
    return pltpu.CompilerParams(dimension_semantics=sem, vmem_limit_bytes=VMEM_LIMIT_BYTES)


class _W:
    def __init__(self, arr, block=None, index=None):
        self.arr = arr
        self.block = tuple(arr.shape) if block is None else tuple(block)
        self.index = (0,) * arr.ndim if index is None else tuple(index)
        self.shape = tuple(b for b in self.block if b is not None)

    def spec(self):
        index = self.index
        return pl.BlockSpec(self.block, lambda *_: index)


def _layer(arr, i, axis=None, part=0, nparts=1):
    block = [None] + list(arr.shape[1:])
    index = [i] + [0] * (arr.ndim - 1)
    if axis is not None:
        block[axis] = arr.shape[axis] // nparts
        index[axis] = part
    return _W(arr, block, index)


def _inproj_kernel(x_ref, w_ref, gq_ref, gkv_ref, cos_ref, sin_ref,
                   hq_ref, ckv_ref, kpe_ref, kcat_ref, z_ref, xbc_ref, dt_ref, *, cols):
    xb = x_ref[...].astype(BF16)

    def proj(name):
        a, b = cols[name]
        return _dot(xb, w_ref[:, a:b])

    hq_ref[...] = (_rms(proj("q")) * gq_ref[...]).astype(hq_ref.dtype)
    ckv = _rms(proj("kv")) * gkv_ref[...]
    ckv_ref[...] = ckv
    kpe = proj("kpa") * cos_ref[...] + proj("kpb") * sin_ref[...]
    kpe_ref[...] = kpe
    nkv = ckv.shape[-1]
    kcat_ref[:, :nkv] = ckv.astype(kcat_ref.dtype)
    kcat_ref[:, nkv:] = kpe.astype(kcat_ref.dtype)
    z_ref[...] = proj("z")
    xbc_ref[...] = proj("xbc")
    dt_ref[...] = proj("dt")


def _inproj(x, w, gq, gkv, cos_t, sin_t, cols, tm, act_dtype):
    T, D = x.shape
    n_tab = cos_t.shape[0] // tm
    widths = {k: b - a for k, (a, b) in cols.items()}
    row = lambda i: (i, 0)
    tab = lambda i: (i % n_tab, 0)
    out_shapes = (
        jax.ShapeDtypeStruct((T, widths["q"]), act_dtype),
        jax.ShapeDtypeStruct((T, widths["kv"]), F32),
        jax.ShapeDtypeStruct((T, widths["kpa"]), F32),
        jax.ShapeDtypeStruct((T, widths["kv"] + widths["kpa"]), act_dtype),
        jax.ShapeDtypeStruct((T, widths["z"]), F32),
        jax.ShapeDtypeStruct((T, widths["xbc"]), F32),
        jax.ShapeDtypeStruct((T, widths["dt"]), F32),
    )
    return pl.pallas_call(
        functools.partial(_inproj_kernel, cols=cols),
        grid=(T // tm,),
        in_specs=[
            pl.BlockSpec((tm, D), row),
            w.spec(), gq.spec(), gkv.spec(),
            pl.BlockSpec((tm, cos_t.shape[1]), tab),
            pl.BlockSpec((tm, sin_t.shape[1]), tab),
        ],
        out_specs=tuple(pl.BlockSpec((tm, s.shape[1]), row) for s in out_shapes),
        out_shape=out_shapes,
        compiler_params=_cparams(("parallel",)),
        name="inproj",
    )(x, w.arr, gq.arr, gkv.arr, cos_t, sin_t)


def _qprep_kernel(hq_ref, wn_ref, wpa_ref, wpb_ref, wuk_ref, cos_ref, sin_ref, qabs_ref, qpe_ref, *, heads):
    hq = hq_ref[...].astype(BF16)
    qn = _dot(hq, wn_ref[...])
    qpe = _dot(hq, wpa_ref[...]) * cos_ref[...] + _dot(hq, wpb_ref[...]) * sin_ref[...]
    qpe_ref[...] = qpe.astype(qpe_ref.dtype)
    per = LANES // NOPE_DIM
    for h in range(heads):
        slab = qn[:, (h // per) * LANES:(h // per + 1) * LANES].astype(BF16)
        qabs_ref[h] = _dot(slab, wuk_ref[h]).astype(qabs_ref.dtype)


def _qprep(hq, wn, wpa, wpb, wuk, cos_t, sin_t, tm, act_dtype):
    T, Q = hq.shape
    heads, _, R = wuk.shape
    n_tab = cos_t.shape[0] // tm
    row = lambda i: (i, 0)
    tab = lambda i: (i % n_tab, 0)
    return pl.pallas_call(
        functools.partial(_qprep_kernel, heads=heads),
        grid=(T // tm,),
        in_specs=[
            pl.BlockSpec((tm, Q), row),
            wn.spec(), wpa.spec(), wpb.spec(), wuk.spec(),
            pl.BlockSpec((tm, cos_t.shape[1]), tab),
            pl.BlockSpec((tm, sin_t.shape[1]), tab),
        ],
        out_specs=(pl.BlockSpec((heads, tm, R), lambda i: (0, i, 0)),
                   pl.BlockSpec((tm, wpa.shape[1]), row)),
        out_shape=(jax.ShapeDtypeStruct((heads, T, R), act_dtype),
                   jax.ShapeDtypeStruct((T, wpa.shape[1]), act_dtype)),
        compiler_params=_cparams(("parallel",)),
        name="qprep",
    )(hq, wn.arr, wpa.arr, wpb.arr, wuk.arr, cos_t, sin_t)


def _attn_kernel(qi_ref, kj_ref, qabs_ref, qpe_ref, k_ref, wuv_ref, o_ref,
                 q_scr, m_scr, l_scr, acc_scr, *, heads, tq, tk, scale):
    s = pl.program_id(1)
    i = qi_ref[s]
    j = kj_ref[s]
    nkv = qabs_ref.shape[-1]
    c2 = scale * math.log2(math.e)

    @pl.when(j == 0)
    def _():
        qpe = qpe_ref[...]
        lane = lax.broadcasted_iota(jnp.int32, qpe.shape, 1)
        for h in range(heads):
            q_scr[h * tq:(h + 1) * tq, :nkv] = qabs_ref[h].astype(BF16)
            q_scr[h * tq:(h + 1) * tq, nkv:] = jnp.where(lane // ROPE_DIM == h, qpe, 0).astype(BF16)
        m_scr[...] = jnp.full(m_scr.shape, -jnp.inf, F32)
        l_scr[...] = jnp.zeros(l_scr.shape, F32)
        acc_scr[...] = jnp.zeros(acc_scr.shape, F32)

    k = k_ref[...]
    cr = 2 * tq

    def update(mask):
        for ch in range(heads * tq // cr):
            rs = slice(ch * cr, (ch + 1) * cr)
            sc = _dot_nt(q_scr[rs, :], k)
            if mask is not None:
                sc = jnp.where(mask, sc, -jnp.inf)
            m_old = m_scr[rs, :]
            m_new = jnp.maximum(m_old, jnp.max(sc, axis=-1, keepdims=True))
            alpha = jnp.exp2((m_old - m_new) * c2)
            p = jnp.exp2((sc - jnp.tile(m_new, (1, tk // LANES))) * c2)
            psum = p[:, :LANES]
            for t in range(1, tk // LANES):
                psum = psum + p[:, t * LANES:(t + 1) * LANES]
            l_scr[rs, :] = alpha * l_scr[rs, :] + psum
            acc_scr[rs, :] = (jnp.tile(alpha, (1, nkv // LANES)) * acc_scr[rs, :]
                              + _dot(p.astype(BF16), k[:, :nkv]))
            m_scr[rs, :] = m_new

    diag = ((i + 1) * tq - 1) // tk

    @pl.when(j < diag)
    def _():
        update(None)

    @pl.when(j == diag)
    def _():
        r = lax.broadcasted_iota(jnp.int32, (cr, tk), 0)
        c = lax.broadcasted_iota(jnp.int32, (cr, tk), 1)
        update(j * tk + c <= i * tq + (r & (tq - 1)))
        o = acc_scr[...] / jnp.sum(l_scr[...], axis=-1, keepdims=True)
        for p in range(heads // 2):
            lo = o[(2 * p) * tq:(2 * p + 1) * tq].astype(BF16)
            hi = o[(2 * p + 1) * tq:(2 * p + 2) * tq].astype(BF16)
            o_ref[:, p * LANES:(p + 1) * LANES] = (
                _dot(lo, wuv_ref[2 * p]) + _dot(hi, wuv_ref[2 * p + 1])).astype(o_ref.dtype)


def _attn_prompt(qabs, qpe, kcat, wuv, B, S, tq, tk, scale, act_dtype):
    heads, T, R = qabs.shape
    assert tk % tq == 0 and S % tk == 0
    nq = S // tq
    nk = S // tk
    pairs = [(i, j) for i in range(nq) for j in range(((i + 1) * tq - 1) // tk + 1)]
    qi = jnp.asarray([p[0] for p in pairs], jnp.int32)
    kj = jnp.asarray([p[1] for p in pairs], jnp.int32)
    kw = kcat.shape[1]
    out_w = (heads // 2) * LANES
    grid_spec = pltpu.PrefetchScalarGridSpec(
        num_scalar_prefetch=2,
        grid=(B, len(pairs)),
        in_specs=[
            pl.BlockSpec((heads, tq, R), lambda b, s, qi, kj: (0, b * nq + qi[s], 0)),
            pl.BlockSpec((tq, qpe.shape[1]), lambda b, s, qi, kj: (b * nq + qi[s], 0)),
            pl.BlockSpec((tk, kw), lambda b, s, qi, kj: (b * nk + kj[s], 0)),
            wuv.spec(),
        ],
        out_specs=pl.BlockSpec((tq, out_w), lambda b, s, qi, kj: (b * nq + qi[s], 0)),
        scratch_shapes=[
            pltpu.VMEM((heads * tq, kw), BF16),
            pltpu.VMEM((heads * tq, LANES), F32),
            pltpu.VMEM((heads * tq, LANES), F32),
            pltpu.VMEM((heads * tq, R), F32),
        ],
    )
    return pl.pallas_call(
        functools.partial(_attn_kernel, heads=heads, tq=tq, tk=tk, scale=scale),
        grid_spec=grid_spec,
        out_shape=jax.ShapeDtypeStruct((T, out_w), act_dtype),
        compiler_params=_cparams(("parallel", "arbitrary")),
        name="attn_prompt",
    )(qi, kj, qabs, qpe, kcat, wuv.arr)


def _attn_sample_kernel(pt_ref, qabs_ref, qpe_ref, knew_ref, wuv_ref, *rest,
                        heads, rows, valid, npages, scale):
    ckv_refs = rest[:npages]
    kpe_refs = rest[npages:2 * npages]
    o_ref = rest[2 * npages]
    qa_scr, qp_scr, m_scr, l_scr, acc_scr = rest[2 * npages + 1:]
    s = pl.program_id(1)
    ns = pl.num_programs(1)
    nkv = qabs_ref.shape[-1]
    c2 = scale * math.log2(math.e)

    @pl.when(s == 0)
    def _():
        qpe = qpe_ref[...]
        for h in range(heads):
            qa_scr[h * rows:(h + 1) * rows, :] = qabs_ref[h]
            qp_scr[h * rows:(h + 1) * rows, :] = qpe[:, h * ROPE_DIM:(h + 1) * ROPE_DIM]
        m_scr[...] = jnp.full(m_scr.shape, -jnp.inf, F32)
        l_scr[...] = jnp.zeros(l_scr.shape, F32)
        acc_scr[...] = jnp.zeros(acc_scr.shape, F32)

    qa = qa_scr[...].astype(BF16)
    qp = qp_scr[...].astype(BF16)

    def update(sc, v):
        m_old = m_scr[...]
        m_new = jnp.maximum(m_old, jnp.max(sc, axis=-1, keepdims=True))
        alpha = jnp.exp2((m_old - m_new) * c2)
        p = jnp.exp2((sc - jnp.tile(m_new, (1, sc.shape[1] // LANES))) * c2)
        psum = p[:, :LANES]
        for t in range(1, sc.shape[1] // LANES):
            psum = psum + p[:, t * LANES:(t + 1) * LANES]
        l_scr[...] = alpha * l_scr[...] + psum
        acc_scr[...] = jnp.tile(alpha, (1, nkv // LANES)) * acc_scr[...] + _dot(p.astype(BF16), v)
        m_scr[...] = m_new

    kb = jnp.concatenate([c_ref[...].astype(BF16) for c_ref in ckv_refs], axis=0)
    pt = jnp.concatenate([p_ref[...] for p_ref in kpe_refs], axis=1).astype(BF16)
    update(_dot_nt(qa, kb) + _dot(qp, pt), kb)

    @pl.when(s == ns - 1)
    def _():
        kn = knew_ref[...]
        kn = jnp.concatenate([kn, jnp.zeros((LANES - rows, kn.shape[1]), kn.dtype)], axis=0).astype(BF16)
        kb = kn[:, :nkv]
        sc = _dot_nt(qa, kb) + _dot_nt(qp, kn[:, nkv:nkv + ROPE_DIM])
        tok = lax.broadcasted_iota(jnp.int32, sc.shape, 0) & (rows - 1)
        key = lax.broadcasted_iota(jnp.int32, sc.shape, 1)
        update(jnp.where((key <= tok) & (key < valid), sc, -jnp.inf), kb)
        ob = (acc_scr[...] / jnp.sum(l_scr[...], axis=-1, keepdims=True)).astype(BF16)
        for p in range(heads // 2):
            lo = _dot(ob, wuv_ref[2 * p])[(2 * p) * rows:(2 * p + 1) * rows]
            hi = _dot(ob, wuv_ref[2 * p + 1])[(2 * p + 1) * rows:(2 * p + 2) * rows]
            o_ref[:, p * LANES:(p + 1) * LANES] = (lo + hi).astype(o_ref.dtype)


def _attn_sample(qabs, qpe, knew, wuv, cache_ckv, cache_kpe, page_table, layer, rows, valid, scale):
    heads, T, R = qabs.shape
    B, n_pages = page_table.shape
    npg = PAGES_PER_STEP
    steps = n_pages // npg
    _, _, page, klat = cache_ckv.shape
    out_w = (heads // 2) * LANES
    kpeT = jnp.swapaxes(cache_kpe, 2, 3)

    def page_spec(shape, p):
        return pl.BlockSpec((None, None) + shape, lambda b, s, pt: (layer, pt[b, s * npg + p], 0, 0))

    grid_spec = pltpu.PrefetchScalarGridSpec(
        num_scalar_prefetch=1,
        grid=(B, steps),
        in_specs=[
            pl.BlockSpec((heads, rows, R), lambda b, s, pt: (0, b, 0)),
            pl.BlockSpec((rows, qpe.shape[1]), lambda b, s, pt: (b, 0)),
            pl.BlockSpec((rows, knew.shape[1]), lambda b, s, pt: (b, 0)),
            wuv.spec(),
        ] + [page_spec((page, klat), p) for p in range(npg)]
        + [page_spec((ROPE_DIM, page), p) for p in range(npg)],
        out_specs=pl.BlockSpec((rows, out_w), lambda b, s, pt: (b, 0)),
        scratch_shapes=[
            pltpu.VMEM((heads * rows, R), F32),
            pltpu.VMEM((heads * rows, ROPE_DIM), F32),
            pltpu.VMEM((heads * rows, LANES), F32),
            pltpu.VMEM((heads * rows, LANES), F32),
            pltpu.VMEM((heads * rows, R), F32),
        ],
    )
    return pl.pallas_call(
        functools.partial(_attn_sample_kernel, heads=heads, rows=rows, valid=valid, npages=npg, scale=scale),
        grid_spec=grid_spec,
        out_shape=jax.ShapeDtypeStruct((T, out_w), F32),
        compiler_params=_cparams(("parallel", "arbitrary")),
        name="attn_sample",
    )(page_table, qabs, qpe, knew, wuv.arr, *([cache_ckv] * npg), *([kpeT] * npg))


def _ssd_kernel(xbc_ref, z_ref, dt_ref, dtT_ref, hist_ref, h0_ref, cw_ref, cb_ref,
                dtb_row_ref, dtb_col_ref, alog_row_ref, alog_col_ref, dskip_ref, gy_ref,
                y_ref, hlast_ref, xp_scr, h_scr, *, l, valid, heads, hdim, nstate, kconv):
    c = pl.program_id(1)
    nc = pl.num_programs(1)
    inner = heads * hdim
    gw = SSM_GROUPS * nstate
    hpg = heads // SSM_GROUPS
    pad = SUBLANES
    lb = xbc_ref.shape[0]

    def pad_rows(a):
        if lb == l:
            return a
        return jnp.concatenate([a, jnp.zeros((l - lb, a.shape[1]), a.dtype)], axis=0)

    @pl.when(c == 0)
    def _():
        xp_scr[0:pad, :] = hist_ref[...]
        h_scr[...] = h0_ref[...]

    xp_scr[pad:pad + l, :] = pad_rows(xbc_ref[...])
    acc = jnp.zeros((l, xbc_ref.shape[1]), F32) + cb_ref[...]
    for k in range(kconv):
        acc = acc + cw_ref[k:k + 1, :] * xp_scr[pl.ds(pad - (kconv - 1) + k, l), :]
    carry = xp_scr[l:l + pad, :]
    xp_scr[0:pad, :] = carry
    xc = _silu(acc)
    xs = xc[:, :inner]
    bm = xc[:, inner:inner + gw]
    cm = xc[:, inner + gw:inner + 2 * gw]

    dt_col = _softplus(pad_rows(dt_ref[...]) + dtb_row_ref[...])
    dt_row = _softplus(dtT_ref[...] + dtb_col_ref[...])
    if valid < l:
        dt_col = jnp.where(lax.broadcasted_iota(jnp.int32, dt_col.shape, 0) < valid, dt_col, 0.0)
        dt_row = jnp.where(lax.broadcasted_iota(jnp.int32, dt_row.shape, 1) < valid, dt_row, 0.0)
    dta_col = dt_col * (-jnp.exp(alog_row_ref[...]))
    dta_row = dt_row * (-jnp.exp(alog_col_ref[...]))

    ri = lax.broadcasted_iota(jnp.int32, (l, l), 0)
    ci = lax.broadcasted_iota(jnp.int32, (l, l), 1)
    causal = ci <= ri
    lower = jnp.where(causal, 1.0, 0.0).astype(BF16)
    upper = jnp.where(ri <= ci, 1.0, 0.0).astype(BF16)
    acs_col = sum(_dot(lower, piece) for piece in _split3(dta_col))
    acs_row = sum(_dot(piece, upper) for piece in _split3(dta_row))
    acs_last = acs_col[l - 1:l, :]
    eacs = jnp.exp(acs_col)
    toend = jnp.exp(acs_last - acs_col)

    lane = lax.broadcasted_iota(jnp.int32, (l, LANES), 1)
    per = LANES // hdim
    srow = lax.broadcasted_iota(jnp.int32, (LANES, nstate), 0)
    ys = []
    for p in range(heads // per):
        g = (p * per) // hpg
        bg = bm[:, g * nstate:(g + 1) * nstate].astype(BF16)
        cg = cm[:, g * nstate:(g + 1) * nstate]
        cbg = _dot_nt(cg.astype(BF16), bg)
        xs_p = xs[:, p * LANES:(p + 1) * LANES]
        dsk_p = dskip_ref[:, p * LANES:(p + 1) * LANES]
        hprev = h_scr[p * LANES:(p + 1) * LANES, :]
        hprev_b = hprev.astype(BF16)
        dt_p = jnp.zeros((l, LANES), F32)
        te_p = jnp.zeros((l, LANES), F32)
        cd_p = jnp.zeros((LANES, nstate), F32)
        for s in range(per):
            h = p * per + s
            sel = (lane // hdim) == s
            dt_p = jnp.where(sel, dt_col[:, h:h + 1], dt_p)
            te_p = jnp.where(sel, toend[:, h:h + 1], te_p)
            cd_p = jnp.where((srow // hdim) == s, jnp.exp(acs_row[h:h + 1, l - 1:l]), cd_p)
        dtx_p = dt_p * xs_p
        y_p = dsk_p * xs_p
        for s in range(per):
            h = p * per + s
            sel = (lane // hdim) == s
            seg = acs_col[:, h:h + 1] - acs_row[h:h + 1, :]
            dec = jnp.exp(jnp.where(causal, seg, -jnp.inf))
            mh = (cbg * dec).astype(BF16)
            y_h = _dot(mh, jnp.where(sel, dtx_p, 0.0).astype(BF16))
            cs = (cg * eacs[:, h:h + 1]).astype(BF16)
            y_h = y_h + jnp.where(sel, _dot_nt(cs, hprev_b), 0.0)
            y_p = y_p + y_h
        st_p = _dot_tn((dtx_p * te_p).astype(BF16), bg)
        h_scr[p * LANES:(p + 1) * LANES, :] = cd_p * hprev + st_p
        ys.append(y_p)
    y = jnp.concatenate(ys, axis=-1)[:lb] * _silu(z_ref[...])
    gsz = inner // SSM_GROUPS
    outs = [_rms(y[:, g * gsz:(g + 1) * gsz]) for g in range(SSM_GROUPS)]
    y_ref[...] = (jnp.concatenate(outs, axis=-1) * gy_ref[...]).astype(y_ref.dtype)

    @pl.when(c == nc - 1)
    def _():
        hlast_ref[...] = h_scr[...]


def _ssd(xbc, z, dt, dtT, hist, h0, wts, B, S, lb, valid, act_dtype):
    T, cdim = xbc.shape
    nc = S // lb
    l = SSD_CHUNK
    assert lb == l or nc == 1
    heads, hdim, nstate = wts["heads"], wts["hdim"], wts["nstate"]
    inner = heads * hdim
    kconv = wts["conv_w"].shape[0]
    row = lambda b, c: (b * nc + c, 0)
    fixed = lambda b, c: (0, 0)
    small = [wts["conv_w"], wts["conv_b"], wts["dtb_row"], wts["dtb_col"], wts["alog_row"], wts["alog_col"],
             wts["dskip"], wts["gy"]]
    return pl.pallas_call(
        functools.partial(_ssd_kernel, l=l, valid=valid, heads=heads, hdim=hdim, nstate=nstate, kconv=kconv),
        grid=(B, nc),
        in_specs=[
            pl.BlockSpec((lb, cdim), row),
            pl.BlockSpec((lb, inner), row),
            pl.BlockSpec((lb, dt.shape[1]), row),
            pl.BlockSpec((None, dtT.shape[1], l), lambda b, c: (b, 0, c)),
            pl.BlockSpec((None, None, SUBLANES, cdim), lambda b, c: (hist[1], b, 0, 0)),
            pl.BlockSpec((None, None, inner, nstate), lambda b, c: (h0[1], b, 0, 0)),
        ] + [a.spec() for a in small],
        out_specs=(pl.BlockSpec((lb, inner), row),
                   pl.BlockSpec((None, inner, nstate), lambda b, c: (b, 0, 0))),
        out_shape=(jax.ShapeDtypeStruct((T, inner), act_dtype),
                   jax.ShapeDtypeStruct((B, inner, nstate), F32)),
        scratch_shapes=[pltpu.VMEM((l + 2 * SUBLANES, cdim), F32),
                        pltpu.VMEM((inner, nstate), F32)],
        compiler_params=_cparams(("parallel", "arbitrary")),
        name="ssd",
    )(xbc, z, dt, dtT, hist[0], h0[0], *[a.arr for a in small])


def _outproj_kernel(a1_ref, a2_ref, w1_ref, w2_ref, x_ref, g_ref, b_ref, o_ref, *, alpha):
    m = _dot(a1_ref[...].astype(BF16), w1_ref[...]) + _dot(a2_ref[...].astype(BF16), w2_ref[...])
    o_ref[...] = _layer_norm(alpha * x_ref[...] + m, g_ref[...], b_ref[...])


def _outproj(a1, a2, w1, w2, x, g, b, alpha, tm):
    T, D = x.shape
    row = lambda i: (i, 0)
    return pl.pallas_call(
        functools.partial(_outproj_kernel, alpha=alpha),
        grid=(T // tm,),
        in_specs=[pl.BlockSpec((tm, a1.shape[1]), row), pl.BlockSpec((tm, a2.shape[1]), row),
                  w1.spec(), w2.spec(), pl.BlockSpec((tm, D), row), g.spec(), b.spec()],
        out_specs=pl.BlockSpec((tm, D), row),
        out_shape=jax.ShapeDtypeStruct((T, D), F32),
        compiler_params=_cparams(("parallel",)),
        name="outproj_ln",
    )(a1, a2, w1.arr, w2.arr, x, g.arr, b.arr)


def _glu_kernel(x_ref, wa_ref, wb_ref, ba_ref, bb_ref, v_ref, *, tn):
    xb = x_ref[...].astype(BF16)
    for c in range(v_ref.shape[1] // tn):
        sl = slice(c * tn, (c + 1) * tn)
        ua = _dot(xb, wa_ref[:, sl]) + ba_ref[:, sl]
        ub = _dot(xb, wb_ref[:, sl]) + bb_ref[:, sl]
        v_ref[:, sl] = ua * jax.nn.sigmoid(ub)


def _glu(x, wa, wb, ba, bb, tm):
    T, D = x.shape
    C = wa.shape[1]
    row = lambda i: (i, 0)
    return pl.pallas_call(
        functools.partial(_glu_kernel, tn=2 * LANES),
        grid=(T // tm,),
        in_specs=[pl.BlockSpec((tm, D), row), wa.spec(), wb.spec(), ba.spec(), bb.spec()],
        out_specs=pl.BlockSpec((tm, C), row),
        out_shape=jax.ShapeDtypeStruct((T, C), F32),
        compiler_params=_cparams(("parallel",)),
        name="conf_glu",
    )(x, wa.arr, wb.arr, ba.arr, bb.arr)


def _conf_kernel(v_ref, hist_ref, dww_ref, dwb_ref, gn_ref, bn_ref, wo_ref, bo_ref, x_ref, g_ref, b_ref,
                 o_ref, xp_scr, conv_scr, *, tm, halo, ktaps, rblk, alpha):
    c = pl.program_id(1)
    nlb = v_ref.shape[1] // LANES

    @pl.when(c == 0)
    def _():
        for cb in range(nlb):
            xp_scr[cb, 0:halo, :] = hist_ref[:, cb * LANES:(cb + 1) * LANES]

    for cb in range(nlb):
        xp_scr[cb, halo:halo + tm, :] = v_ref[:, cb * LANES:(cb + 1) * LANES]
    first = halo - (ktaps - 1)

    def lane_block(cb, carry):
        for rb in range(tm // rblk):
            acc = jnp.zeros((rblk, LANES), F32) + dwb_ref[cb]
            for k in range(ktaps):
                acc = acc + dww_ref[cb, k:k + 1, :] * xp_scr[cb, pl.ds(rb * rblk + first + k, rblk), :]
            conv_scr[cb, rb * rblk:(rb + 1) * rblk, :] = acc
        tail = xp_scr[cb, tm:tm + halo, :]
        xp_scr[cb, 0:halo, :] = tail
        return carry

    lax.fori_loop(0, nlb, lane_block, 0)
    conv = jnp.concatenate([conv_scr[cb] for cb in range(nlb)], axis=-1)
    cn = _silu(_layer_norm(conv, gn_ref[...], bn_ref[...]))
    if tm < 2 * SUBLANES:
        cn = jnp.concatenate([cn, jnp.zeros_like(cn)], axis=0)
    m = _dot(cn.astype(BF16), wo_ref[...])[:tm] + bo_ref[...]
    o_ref[...] = _layer_norm(alpha * x_ref[...] + m, g_ref[...], b_ref[...])


def _conf(v, hist, dww, dwb, gn, bn, wo, bo, x, g, b, B, S, tm, alpha):
    T, C = v.shape
    D = x.shape[1]
    nc = S // tm
    halo = hist[0].shape[2]
    nlb, ktaps, _ = dww.shape
    row = lambda bb, c: (bb * nc + c, 0)
    small = [dww, dwb, gn, bn, wo, bo]
    return pl.pallas_call(
        functools.partial(_conf_kernel, tm=tm, halo=halo, ktaps=ktaps, rblk=min(tm, 64), alpha=alpha),
        grid=(B, nc),
        in_specs=[pl.BlockSpec((tm, C), row),
                  pl.BlockSpec((None, None, halo, C), lambda bb, c: (hist[1], bb, 0, 0))]
        + [a.spec() for a in small]
        + [pl.BlockSpec((tm, D), row), g.spec(), b.spec()],
        out_specs=pl.BlockSpec((tm, D), row),
        out_shape=jax.ShapeDtypeStruct((T, D), F32),
        scratch_shapes=[pltpu.VMEM((nlb, tm + halo, LANES), F32), pltpu.VMEM((nlb, tm, LANES), F32)],
        compiler_params=_cparams(("parallel", "arbitrary")),
        name="conf_conv",
    )(v, hist[0], *[a.arr for a in small], x, g.arr, b.arr)


def _router_kernel(x_ref, wh_ref, wl_ref, rb_ref, g_ref, *, n_exp):
    x = x_ref[...]
    xh = x.astype(BF16)
    xl = (x - xh.astype(F32)).astype(BF16)
    logits = _dot_nt(wh_ref[...], xh) + _dot_nt(wl_ref[...], xh) + _dot_nt(wh_ref[...], xl)
    probs = jax.nn.sigmoid(logits)
    sel = probs + rb_ref[...]
    per = n_exp // N_GROUPS
    s = [sel[e:e + 1, :] for e in range(n_exp)]
    top = []
    for e in range(n_exp):
        g0 = (e // per) * per
        rank = jnp.zeros(s[e].shape, F32)
        for j in range(g0, g0 + per):
            if j < e:
                rank = rank + jnp.where(s[j] >= s[e], 1.0, 0.0)
            elif j > e:
                rank = rank + jnp.where(s[j] > s[e], 1.0, 0.0)
        top.append(rank < TOP_K)
    grp = []
    for g in range(N_GROUPS):
        tot = jnp.zeros(s[0].shape, F32)
        for e in range(g * per, (g + 1) * per):
            tot = tot + jnp.where(top[e], s[e], 0.0)
        grp.append(tot)
    chosen = []
    for g in range(N_GROUPS):
        ok = None
        for j in range(N_GROUPS):
            if j == g:
                continue
            t = (grp[j] < grp[g]) if j < g else (grp[j] <= grp[g])
            ok = t if ok is None else (ok & t)
        chosen.append(ok)
    w = [jnp.where(chosen[e // per] & top[e], probs[e:e + 1, :], 0.0) for e in range(n_exp)]
    den = w[0]
    for e in range(1, n_exp):
        den = den + w[e]
    for e in range(n_exp):
        g_ref[e:e + 1, :] = w[e] / den


def _router(x, wh, wl, rb, tm):
    T, D = x.shape
    E = wh.shape[0]
    return pl.pallas_call(
        functools.partial(_router_kernel, n_exp=E),
        grid=(T // tm,),
        in_specs=[pl.BlockSpec((tm, D), lambda i: (i, 0)), wh.spec(), wl.spec(), rb.spec()],
        out_specs=pl.BlockSpec((E, tm), lambda i: (0, i)),
        out_shape=jax.ShapeDtypeStruct((E, T), F32),
        compiler_params=_cparams(("parallel",)),
        name="router",
    )(x, wh.arr, wl.arr, rb.arr)


def _moe_kernel(x_ref, gates_ref, wg_ref, wu_ref, wd_ref, g_ref, b_ref, o_ref, xb_scr, acc_scr, *, alpha):
    grp = pl.program_id(1)
    ngrp = pl.num_programs(1)
    per, F, D = wd_ref.shape

    @pl.when(grp == 0)
    def _():
        xb_scr[...] = x_ref[...].astype(BF16)
        acc_scr[...] = jnp.zeros(acc_scr.shape, F32)

    xb = xb_scr[...]
    gates = gates_ref[...]
    lane = lax.broadcasted_iota(jnp.int32, gates.shape, 1)
    hs = []
    for j in range(per):
        gate = jnp.sum(jnp.where(lane == grp * per + j, gates, 0.0), axis=-1, keepdims=True)
        h = _silu(_dot(xb, wg_ref[j])) * _dot(xb, wu_ref[j])
        hs.append((h * gate).astype(BF16))
    acc_scr[...] += _dot(jnp.concatenate(hs, axis=-1), wd_ref[...].reshape(per * F, D))

    @pl.when(grp == ngrp - 1)
    def _():
        o_ref[...] = _layer_norm(alpha * x_ref[...] + acc_scr[...], g_ref[...], b_ref[...])


def _moe(x, gates, wg, wu, wd, g, b, layer, alpha, tm):
    T, D = x.shape
    _, E, _, F = wg.shape
    per = E // N_GROUPS
    row = lambda i, e: (i, 0)
    fixed = lambda i, e: (0, 0)
    return pl.pallas_call(
        functools.partial(_moe_kernel, alpha=alpha),
        grid=(T // tm, N_GROUPS),
        in_specs=[pl.BlockSpec((tm, D), row), pl.BlockSpec((tm, E), row),
                  pl.BlockSpec((None, per, D, F), lambda i, e: (layer, e, 0, 0)),
                  pl.BlockSpec((None, per, D, F), lambda i, e: (layer, e, 0, 0)),
                  pl.BlockSpec((None, per, F, D), lambda i, e: (layer, e, 0, 0)),
                  g.spec(), b.spec()],
        out_specs=pl.BlockSpec((tm, D), row),
        out_shape=jax.ShapeDtypeStruct((T, D), F32),
        scratch_shapes=[pltpu.VMEM((tm, D), BF16), pltpu.VMEM((tm, D), F32)],
        compiler_params=_cparams(("parallel", "arbitrary")),
        name="moe",
    )(x, gates, wg, wu, wd, g.arr, b.arr)


def _rope_tables(pos, heads, reps):
    half = ROPE_DIM // 2
    inv = ROPE_BASE ** (-jnp.arange(half, dtype=F32) / half)
    ang = pos.astype(F32)[:, None] * inv[None, :]
    cos, sin = jnp.cos(ang), jnp.sin(ang)
    cos_t = jnp.tile(jnp.concatenate([cos, cos], -1), (reps, heads))
    sin_t = jnp.tile(jnp.concatenate([-sin, sin], -1), (reps, heads))
    return cos_t, sin_t


def _swap_halves(w):
    half = ROPE_DIM // 2
    return jnp.concatenate([w[..., half:], w[..., :half]], axis=-1)


def _prep_ab(w_in, g_q, g_kv, w_uq, w_uk, w_uv, conv_w, conv_b, dt_bias, a_log, d_skip, g_y, w_out):
    n_a, q_lora = g_q.shape
    kv_lora = g_kv.shape[1]
    heads = w_uq.shape[2]
    s_heads = dt_bias.shape[1]
    cdim = conv_w.shape[2]
    o1 = q_lora
    o2 = o1 + kv_lora
    o3 = o2 + ROPE_DIM
    inner = w_in.shape[2] - o3 - cdim - s_heads
    o4 = o3 + inner
    o5 = o4 + cdim
    hdim = inner // s_heads
    nstate = (cdim - inner) // (2 * SSM_GROUPS)
    wkp = w_in[:, :, o2:o3]
    dt_pad = jnp.pad(w_in[:, :, o5:], ((0, 0), (0, 0), (0, LANES - s_heads)))
    pieces = [("q", w_in[:, :, :o1]), ("kv", w_in[:, :, o1:o2]), ("kpa", jnp.tile(wkp, (1, 1, heads))),
              ("kpb", jnp.tile(_swap_halves(wkp), (1, 1, heads))), ("z", w_in[:, :, o3:o4]),
              ("xbc", w_in[:, :, o4:o5]), ("dt", dt_pad)]
    cols, off = {}, 0
    for name, p in pieces:
        cols[name] = (off, off + p.shape[2])
        off += p.shape[2]
    w_all = jnp.concatenate([p for _, p in pieces], axis=2).astype(BF16)

    wn = w_uq[..., :NOPE_DIM].reshape(n_a, q_lora, heads * NOPE_DIM).astype(BF16)
    wpe = w_uq[..., NOPE_DIM:]
    wpa = wpe.reshape(n_a, q_lora, heads * ROPE_DIM).astype(BF16)
    wpb = _swap_halves(wpe).reshape(n_a, q_lora, heads * ROPE_DIM).astype(BF16)
    per = LANES // NOPE_DIM
    slot = (jnp.arange(heads)[:, None] % per == jnp.arange(per)[None, :]).astype(F32)
    ukt = jnp.transpose(w_uk, (0, 2, 3, 1))
    wuk = (ukt[:, :, None] * slot[None, :, :, None, None]).reshape(n_a, heads, LANES, kv_lora).astype(BF16)
    v_dim = w_uv.shape[3]
    vper = LANES // v_dim
    vslot = (jnp.arange(heads)[:, None] % vper == jnp.arange(vper)[None, :]).astype(F32)
    uvt = jnp.transpose(w_uv, (0, 2, 1, 3))
    wuv = (uvt[:, :, :, None] * vslot[None, :, None, :, None]).reshape(n_a, heads, kv_lora, LANES).astype(BF16)

    def row(v, width):
        return jnp.pad(v, ((0, 0), (0, width - v.shape[1])))[:, None, :]

    def col(v, height):
        return jnp.pad(v, ((0, 0), (0, height - v.shape[1])))[:, :, None]

    ssd = dict(conv_w=conv_w, conv_b=conv_b[:, None, :],
               dtb_row=row(dt_bias, LANES), dtb_col=col(dt_bias, 2 * SUBLANES),
               alog_row=row(a_log, LANES), alog_col=col(a_log, 2 * SUBLANES),
               dskip=jnp.repeat(d_skip, hdim, axis=1)[:, None, :], gy=g_y[:, None, :])
    return dict(cols=cols, w_all=w_all, gq=g_q[:, None, :], gkv=g_kv[:, None, :], wn=wn, wpa=wpa, wpb=wpb,
                wuk=wuk, wuv=wuv, ssd=ssd, w_out=w_out.astype(BF16), n_mla=heads * v_dim,
                dims=dict(heads=s_heads, hdim=hdim, nstate=nstate))


def _ab_layer(st, i):
    lay = lambda name: _layer(st[name], i)
    n_out = st["w_out"].shape[1]
    assert 2 * st["n_mla"] == n_out
    ssd = dict(st["dims"], **{k: _layer(v, i) for k, v in st["ssd"].items()})
    return dict(cols=st["cols"], w_all=lay("w_all"), gq=lay("gq"), gkv=lay("gkv"), wn=lay("wn"), wpa=lay("wpa"),
                wpb=lay("wpb"), wuk=lay("wuk"), wuv=lay("wuv"), ssd=ssd,
                w_out_a=_layer(st["w_out"], i, axis=1, part=0, nparts=2),
                w_out_b=_layer(st["w_out"], i, axis=1, part=1, nparts=2))


def _mixer_ab(x, grp, wts, tabs, ln_g, ln_b, alpha, paged=None):
    B, S, tm, l, valid, act = grp["B"], grp["S"], grp["tm"], grp["l"], grp["valid"], grp["act"]
    cos_t, sin_t = tabs
    hq, ckv, kpe, kcat, z, xbc, dt = _inproj(x, wts["w_all"], wts["gq"], wts["gkv"], cos_t, sin_t,
                                            wts["cols"], tm, act)
    qabs, qpe = _qprep(hq, wts["wn"], wts["wpa"], wts["wpb"], wts["wuk"], cos_t, sin_t, tm, act)
    scale = (NOPE_DIM + ROPE_DIM) ** -0.5
    if paged is None:
        o_mla = _attn_prompt(qabs, qpe, kcat, wts["wuv"], B, S, grp["tq"], grp["tk"], scale, act)
    else:
        cache_ckv, cache_kpe, page_table, idx = paged
        o_mla = _attn_sample(qabs, qpe, kcat, wts["wuv"], cache_ckv, cache_kpe, page_table, idx, S, valid, scale)
    s_heads = wts["ssd"]["heads"]
    dtT = jnp.swapaxes(dt[:, :s_heads].reshape(B, S, s_heads), 1, 2)
    dtT = jnp.pad(dtT, ((0, 0), (0, 2 * SUBLANES - s_heads), (0, max(SSD_CHUNK - S, 0))))
    y, h_last = _ssd(xbc, z, dt, dtT, grp["ssm_hist"], grp["ssm_h0"], wts["ssd"], B, S, l, valid, act)
    x_new = _outproj(o_mla, y, wts["w_out_a"], wts["w_out_b"], x, ln_g, ln_b, alpha, tm)
    return x_new, ckv, kpe, h_last, xbc


def _mixer_c(x, grp, wts, ln_g, ln_b, alpha):
    B, S, tm = grp["B"], grp["S"], grp["tm"]
    v = _glu(x, wts["wa"], wts["wb"], wts["ba"], wts["bb"], tm)
    x_new = _conf(v, grp["conf_hist"], wts["dww"], wts["dwb"], wts["gn"], wts["bn"], wts["wo"], wts["bo"],
                  x, ln_g, ln_b, B, S, min(tm, S), alpha)
    return x_new, v


def _moe_block(x, rt, wg, wu, wd, ln_g, ln_b, layer, alpha, tm_r, tm_m):
    gates_t = _router(x, rt["wh"], rt["wl"], rt["rb"], tm_r)
    return _moe(x, gates_t.T, wg, wu, wd, ln_g, ln_b, layer, alpha, tm_m)


def kernel(x_prompt, x_sample, cache_ckv, cache_kpe, state_ssm, state_ssm_conv, state_conf_conv, page_table,
           w_in_ab, g_q_norm, g_kv_norm, w_uq, w_uk, w_uv, ssm_conv_w, ssm_conv_b, ssm_dt_bias, ssm_a_log,
           ssm_d, ssm_norm_g, w_out_ab, conf_w_in, conf_b_in, conf_dw_w, conf_dw_b, conf_norm_g, conf_norm_b,
           conf_w_out, conf_b_out, ln_mix_g, ln_mix_b, ln_ffn_g, ln_ffn_b, w_router, router_bias,
           moe_w_gate, moe_w_up, moe_w_down):
    bp, S, D = x_prompt.shape
    bd, t_new, _ = x_sample.shape
    depth = ln_mix_g.shape[0]
    past_len = page_table.shape[1] * PAGE_SIZE
    alpha = (2 * depth) ** 0.25
    heads = w_uq.shape[2]
    n_a = w_in_ab.shape[0]
    n_c = conf_w_in.shape[0]
    conf_ch = conf_dw_w.shape[2]
    conf_k = conf_dw_w.shape[1]
    kconv = ssm_conv_w.shape[1]
    cdim = ssm_conv_w.shape[2]
    rows = SAMPLE_ROWS
    tm_p = 256

    ab_st = _prep_ab(w_in_ab, g_q_norm, g_kv_norm, w_uq, w_uk, w_uv, ssm_conv_w, ssm_conv_b, ssm_dt_bias,
                     ssm_a_log, ssm_d, ssm_norm_g, w_out_ab)
    ab = [_ab_layer(ab_st, i) for i in range(n_a)]
    nlb = conf_ch // LANES
    cw_in = conf_w_in.astype(BF16)
    cb_in = conf_b_in[:, None, :]
    cw_dw = jnp.transpose(conf_dw_w.reshape(n_c, conf_k, nlb, LANES), (0, 2, 1, 3))
    cb_dw = conf_dw_b.reshape(n_c, nlb, 1, LANES)
    cw_out = conf_w_out.astype(BF16)
    c_gn, c_bn, cb_out = conf_norm_g[:, None, :], conf_norm_b[:, None, :], conf_b_out[:, None, :]
    cw = [dict(wa=_layer(cw_in, i, axis=2, part=0, nparts=2), wb=_layer(cw_in, i, axis=2, part=1, nparts=2),
               ba=_layer(cb_in, i, axis=2, part=0, nparts=2), bb=_layer(cb_in, i, axis=2, part=1, nparts=2),
               dww=_layer(cw_dw, i), dwb=_layer(cb_dw, i), gn=_layer(c_gn, i), bn=_layer(c_bn, i),
               wo=_layer(cw_out, i), bo=_layer(cb_out, i))
          for i in range(n_c)]
    wr_t = w_router.T
    wr_h = wr_t.astype(BF16)
    rt = dict(wh=_W(wr_h), wl=_W((wr_t - wr_h.astype(F32)).astype(BF16)), rb=_W(router_bias[:, None]))
    wg_b, wu_b, wd_b = moe_w_gate.astype(BF16), moe_w_up.astype(BF16), moe_w_down.astype(BF16)
    ln_mg, ln_mb = ln_mix_g[:, None, :], ln_mix_b[:, None, :]
    ln_fg, ln_fb = ln_ffn_g[:, None, :], ln_ffn_b[:, None, :]

    halo = 4 * SUBLANES
    sub_heads = ssm_dt_bias.shape[1]
    inner = state_ssm.shape[2] * state_ssm.shape[3]
    nstate = state_ssm.shape[4]
    gp = dict(B=bp, S=S, tm=tm_p, tq=256, tk=512, l=min(SSD_CHUNK, S), valid=min(SSD_CHUNK, S), act=BF16,
              ssm_hist=(jnp.zeros((1, bp, SUBLANES, cdim), F32), 0),
              ssm_h0=(jnp.zeros((1, bp, inner, nstate), F32), 0),
              conf_hist=(jnp.zeros((1, bp, halo, conf_ch), F32), 0))
    ssm_hist_s = jnp.pad(state_ssm_conv, ((0, 0), (0, 0), (SUBLANES - (kconv - 1), 0), (0, 0)))
    ssm_h0_s = state_ssm.reshape(n_a, bd, inner, nstate)
    conf_hist_s = jnp.pad(state_conf_conv, ((0, 0), (0, 0), (halo - (conf_k - 1), 0), (0, 0)))
    gs = dict(B=bd, S=rows, tm=bd * rows, l=rows, valid=t_new, act=F32)
    tabs_p = _rope_tables(jnp.arange(S, dtype=jnp.int32), heads, 1)
    pos_s = past_len + jnp.arange(rows, dtype=jnp.int32)
    tabs_s = _rope_tables(pos_s, heads, bd)

    hp = x_prompt.reshape(bp * S, D)
    hs = jnp.pad(x_sample, ((0, 0), (0, rows - t_new), (0, 0))).reshape(bd * rows, D)
    out_p = dict(ckv=[], kpe=[], ssm=[], sconv=[], cconv=[])
    out_s = dict(ckv=[], kpe=[], ssm=[], sconv=[], cconv=[])
    for layer in range(depth):
        i = layer // 2
        lg, lb = _layer(ln_mg, layer), _layer(ln_mb, layer)
        if layer % 2 == 0:
            hp, c1, k1, h1, xbc1 = _mixer_ab(hp, gp, ab[i], tabs_p, lg, lb, alpha)
            gs_l = dict(gs, ssm_hist=(ssm_hist_s, i), ssm_h0=(ssm_h0_s, i))
            hs, c2, k2, h2, xbc2 = _mixer_ab(hs, gs_l, ab[i], tabs_s, lg, lb, alpha,
                                             paged=(cache_ckv, cache_kpe, page_table, i))
            out_p["ckv"].append(c1.reshape(bp, S, -1))
            out_p["kpe"].append(k1.reshape(bp, S, -1)[..., :ROPE_DIM])
            out_p["ssm"].append(h1.reshape(bp, sub_heads, -1, nstate))
            out_p["sconv"].append(xbc1.reshape(bp, S, cdim)[:, S - (kconv - 1):])
            out_s["ckv"].append(c2.reshape(bd, rows, -1)[:, :t_new])
            out_s["kpe"].append(k2.reshape(bd, rows, -1)[:, :t_new, :ROPE_DIM])
            out_s["ssm"].append(h2.reshape(bd, sub_heads, -1, nstate))
            xp = jnp.concatenate([state_ssm_conv[i], xbc2.reshape(bd, rows, cdim)[:, :t_new]], axis=1)
            out_s["sconv"].append(xp[:, xp.shape[1] - (kconv - 1):])
        else:
            hp, v1 = _mixer_c(hp, gp, cw[i], lg, lb, alpha)
            gs_l = dict(gs, conf_hist=(conf_hist_s, i))
            hs, v2 = _mixer_c(hs, gs_l, cw[i], lg, lb, alpha)
            out_p["cconv"].append(v1.reshape(bp, S, conf_ch)[:, S - (conf_k - 1):])
            vp = jnp.concatenate([state_conf_conv[i], v2.reshape(bd, rows, conf_ch)[:, :t_new]], axis=1)
            out_s["cconv"].append(vp[:, vp.shape[1] - (conf_k - 1):])
        fg, fb = _layer(ln_fg, layer), _layer(ln_fb, layer)
        hp = _moe_block(hp, rt, wg_b, wu_b, wd_b, fg, fb, layer, alpha, 512, 1024)
        hs = _moe_block(hs, rt, wg_b, wu_b, wd_b, fg, fb, layer, alpha, bd * rows, bd * rows)
    y_p = hp.reshape(bp, S, D)
    y_s = hs.reshape(bd, rows, D)[:, :t_new]
    return (y_p, y_s,
            jnp.stack(out_p["ckv"]), jnp.stack(out_p["kpe"]), jnp.stack(out_p["ssm"]),
            jnp.stack(out_p["sconv"]), jnp.stack(out_p["cconv"]),
            jnp.stack(out_s["ckv"]), jnp.stack(out_s["kpe"]), jnp.stack(out_s["ssm"]),
            jnp.stack(out_s["sconv"]), jnp.stack(out_s["cconv"]))
```

```python
import functools
import math

import jax
import jax.numpy as jnp
from jax import lax
from jax.experimental import pallas as pl
from jax.experimental.pallas import tpu as pltpu

F32 = jnp.float32
BF16 = jnp.bfloat16

PAGE_SIZE = 128
NOPE_DIM = 64
ROPE_DIM = 32
ROPE_BASE = 10000.0
SSM_GROUPS = 2
SSD_CHUNK = 128
N_GROUPS = 4
TOP_K = 2
EPS = 1e-6

LANES = 128
SUBLANES = 8
VMEM_LIMIT_BYTES = 56 * 1024 * 1024
SAMPLE_ROWS = SUBLANES
PAGES_PER_STEP = 16
SAMPLE_CHAINS = 2


def _dot(a, b):
    return jnp.dot(a, b, preferred_element_type=F32)


def _dot_nt(a, b):
    return lax.dot_general(a, b, (((1,), (1,)), ((), ())), preferred_element_type=F32)


def _dot_tn(a, b):
    return lax.dot_general(a, b, (((0,), (0,)), ((), ())), preferred_element_type=F32)


def _rms(x):
    return x * lax.rsqrt(jnp.mean(x * x, axis=-1, keepdims=True) + EPS)


def _layer_norm(r, g, b):
    mu = jnp.mean(r, axis=-1, keepdims=True)
    d = r - mu
    var = jnp.mean(d * d, axis=-1, keepdims=True)
    return d * lax.rsqrt(var + EPS) * g + b


def _silu(x):
    return x * jax.nn.sigmoid(x)


def _softplus(x):
    return jnp.maximum(x, 0.0) + jnp.log1p(jnp.exp(-jnp.abs(x)))


def _split3(x):
    hi = x.astype(BF16)
    r1 = x - hi.astype(F32)
    mid = r1.astype(BF16)
    lo = (r1 - mid.astype(F32)).astype(BF16)
    return hi, mid, lo


def _cparams(sem):
    return pltpu.CompilerParams(dimension_semantics=sem, vmem_limit_bytes=VMEM_LIMIT_BYTES)


class _W:
    def __init__(self, arr, block=None, index=None):
        self.arr = arr
        self.block = tuple(arr.shape) if block is None else tuple(block)
        self.index = (0,) * arr.ndim if index is None else tuple(index)
        self.shape = tuple(b for b in self.block if b is not None)

    def spec(self):
        index = self.index
        return pl.BlockSpec(self.block, lambda *_: index)


def _layer(arr, i, axis=None, part=0, nparts=1):
    block = [None] + list(arr.shape[1:])
    index = [i] + [0] * (arr.ndim - 1)
    if axis is not None:
        block[axis] = arr.shape[axis] // nparts
        index[axis] = part
    return _W(arr, block, index)


def _inproj_kernel(x_ref, w_ref, gq_ref, gkv_ref, cos_ref, sin_ref,
                   hq_ref, ckv_ref, kpe_ref, kcat_ref, z_ref, xbc_ref, dt_ref, *, cols):
    xb = x_ref[...].astype(BF16)

    def proj(name):
        a, b = cols[name]
        return _dot(xb, w_ref[:, a:b])

    hq_ref[...] = (_rms(proj("q")) * gq_ref[...]).astype(hq_ref.dtype)
    ckv = _rms(proj("kv")) * gkv_ref[...]
    ckv_ref[...] = ckv
    kpe = proj("kpa") * cos_ref[...] + proj("kpb") * sin_ref[...]
    kpe_ref[...] = kpe
    nkv = ckv.shape[-1]
    kcat_ref[:, :nkv] = ckv.astype(kcat_ref.dtype)
    kcat_ref[:, nkv:] = kpe.astype(kcat_ref.dtype)
    z_ref[...] = proj("z")
    xbc_ref[...] = proj("xbc")
    dt_ref[...] = proj("dt")


def _inproj(x, w, gq, gkv, cos_t, sin_t, cols, tm, act_dtype):
    T, D = x.shape
    n_tab = cos_t.shape[0] // tm
    widths = {k: b - a for k, (a, b) in cols.items()}
    row = lambda i: (i, 0)
    tab = lambda i: (i % n_tab, 0)
    out_shapes = (
        jax.ShapeDtypeStruct((T, widths["q"]), act_dtype),
        jax.ShapeDtypeStruct((T, widths["kv"]), F32),
        jax.ShapeDtypeStruct((T, widths["kpa"]), F32),
        jax.ShapeDtypeStruct((T, widths["kv"] + widths["kpa"]), act_dtype),
        jax.ShapeDtypeStruct((T, widths["z"]), F32),
        jax.ShapeDtypeStruct((T, widths["xbc"]), F32),
        jax.ShapeDtypeStruct((T, widths["dt"]), F32),
    )
    return pl.pallas_call(
        functools.partial(_inproj_kernel, cols=cols),
        grid=(T // tm,),
        in_specs=[
            pl.BlockSpec((tm, D), row),
            w.spec(), gq.spec(), gkv.spec(),
            pl.BlockSpec((tm, cos_t.shape[1]), tab),
            pl.BlockSpec((tm, sin_t.shape[1]), tab),
        ],
        out_specs=tuple(pl.BlockSpec((tm, s.shape[1]), row) for s in out_shapes),
        out_shape=out_shapes,
        compiler_params=_cparams(("parallel",)),
        name="inproj",
    )(x, w.arr, gq.arr, gkv.arr, cos_t, sin_t)


def _qprep_kernel(hq_ref, wn_ref, wpa_ref, wpb_ref, wuk_ref, cos_ref, sin_ref, qabs_ref, qpe_ref, *, heads):
    hq = hq_ref[...].astype(BF16)
    qn = _dot(hq, wn_ref[...])
    qpe = _dot(hq, wpa_ref[...]) * cos_ref[...] + _dot(hq, wpb_ref[...]) * sin_ref[...]
    qpe_ref[...] = qpe.astype(qpe_ref.dtype)
    per = LANES // NOPE_DIM
    for h in range(heads):
        slab = qn[:, (h // per) * LANES:(h // per + 1) * LANES].astype(BF16)
        qabs_ref[h] = _dot(slab, wuk_ref[h]).astype(qabs_ref.dtype)


def _qprep(hq, wn, wpa, wpb, wuk, cos_t, sin_t, tm, act_dtype):
    T, Q = hq.shape
    heads, _, R = wuk.shape
    n_tab = cos_t.shape[0] // tm
    row = lambda i: (i, 0)
    tab = lambda i: (i % n_tab, 0)
    return pl.pallas_call(
        functools.partial(_qprep_kernel, heads=heads),
        grid=(T // tm,),
        in_specs=[
            pl.BlockSpec((tm, Q), row),
            wn.spec(), wpa.spec(), wpb.spec(), wuk.spec(),
            pl.BlockSpec((tm, cos_t.shape[1]), tab),
            pl.BlockSpec((tm, sin_t.shape[1]), tab),
        ],
        out_specs=(pl.BlockSpec((heads, tm, R), lambda i: (0, i, 0)),
                   pl.BlockSpec((tm, wpa.shape[1]), row)),
        out_shape=(jax.ShapeDtypeStruct((heads, T, R), act_dtype),
                   jax.ShapeDtypeStruct((T, wpa.shape[1]), act_dtype)),
        compiler_params=_cparams(("parallel",)),
        name="qprep",
    )(hq, wn.arr, wpa.arr, wpb.arr, wuk.arr, cos_t, sin_t)


def _attn_kernel(qi_ref, kj_ref, qabs_ref, qpe_ref, k_ref, wuv_ref, o_ref,
                 q_scr, m_scr, l_scr, acc_scr, *, heads, tq, tk, scale):
    s = pl.program_id(1)
    i = qi_ref[s]
    j = kj_ref[s]
    nkv = qabs_ref.shape[-1]
    c2 = scale * math.log2(math.e)

    @pl.when(j == 0)
    def _():
        qpe = qpe_ref[...]
        lane = lax.broadcasted_iota(jnp.int32, qpe.shape, 1)
        for h in range(heads):
            q_scr[h * tq:(h + 1) * tq, :nkv] = qabs_ref[h].astype(BF16)
            q_scr[h * tq:(h + 1) * tq, nkv:] = jnp.where(lane // ROPE_DIM == h, qpe, 0).astype(BF16)
        m_scr[...] = jnp.full(m_scr.shape, -jnp.inf, F32)
        l_scr[...] = jnp.zeros(l_scr.shape, F32)
        acc_scr[...] = jnp.zeros(acc_scr.shape, F32)

    k = k_ref[...]

    def update(sc):
        m_old = m_scr[...]
        m_new = jnp.maximum(m_old, jnp.max(sc, axis=-1, keepdims=True))
        alpha = jnp.exp2((m_old - m_new) * c2)
        p = jnp.exp2((sc - jnp.tile(m_new, (1, tk // LANES))) * c2)
        psum = p[:, :LANES]
        for t in range(1, tk // LANES):
            psum = psum + p[:, t * LANES:(t + 1) * LANES]
        l_scr[...] = alpha * l_scr[...] + psum
        acc_scr[...] = jnp.tile(alpha, (1, nkv // LANES)) * acc_scr[...] + _dot(p.astype(BF16), k[:, :nkv])
        m_scr[...] = m_new

    diag = ((i + 1) * tq - 1) // tk

    @pl.when(j < diag)
    def _():
        update(_dot_nt(q_scr[...], k))

    @pl.when(j == diag)
    def _():
        sc = _dot_nt(q_scr[...], k)
        r = lax.broadcasted_iota(jnp.int32, sc.shape, 0)
        c = lax.broadcasted_iota(jnp.int32, sc.shape, 1)
        update(jnp.where(j * tk + c <= i * tq + (r & (tq - 1)), sc, -jnp.inf))
        o = acc_scr[...] / jnp.sum(l_scr[...], axis=-1, keepdims=True)
        for p in range(heads // 2):
            lo = o[(2 * p) * tq:(2 * p + 1) * tq].astype(BF16)
            hi = o[(2 * p + 1) * tq:(2 * p + 2) * tq].astype(BF16)
            o_ref[:, p * LANES:(p + 1) * LANES] = (
                _dot(lo, wuv_ref[2 * p]) + _dot(hi, wuv_ref[2 * p + 1])).astype(o_ref.dtype)


def _attn_prompt(qabs, qpe, kcat, wuv, B, S, tq, tk, scale, act_dtype):
    heads, T, R = qabs.shape
    assert tk % tq == 0 and S % tk == 0
    nq = S // tq
    nk = S // tk
    pairs = [(i, j) for i in range(nq) for j in range(((i + 1) * tq - 1) // tk + 1)]
    qi = jnp.asarray([p[0] for p in pairs], jnp.int32)
    kj = jnp.asarray([p[1] for p in pairs], jnp.int32)
    kw = kcat.shape[1]
    out_w = (heads // 2) * LANES
    grid_spec = pltpu.PrefetchScalarGridSpec(
        num_scalar_prefetch=2,
        grid=(B, len(pairs)),
        in_specs=[
            pl.BlockSpec((heads, tq, R), lambda b, s, qi, kj: (0, b * nq + qi[s], 0)),
            pl.BlockSpec((tq, qpe.shape[1]), lambda b, s, qi, kj: (b * nq + qi[s], 0)),
            pl.BlockSpec((tk, kw), lambda b, s, qi, kj: (b * nk + kj[s], 0)),
            wuv.spec(),
        ],
        out_specs=pl.BlockSpec((tq, out_w), lambda b, s, qi, kj: (b * nq + qi[s], 0)),
        scratch_shapes=[
            pltpu.VMEM((heads * tq, kw), BF16),
            pltpu.VMEM((heads * tq, LANES), F32),
            pltpu.VMEM((heads * tq, LANES), F32),
            pltpu.VMEM((heads * tq, R), F32),
        ],
    )
    return pl.pallas_call(
        functools.partial(_attn_kernel, heads=heads, tq=tq, tk=tk, scale=scale),
        grid_spec=grid_spec,
        out_shape=jax.ShapeDtypeStruct((T, out_w), act_dtype),
        compiler_params=_cparams(("parallel", "arbitrary")),
        name="attn_prompt",
    )(qi, kj, qabs, qpe, kcat, wuv.arr)


def _attn_sample_kernel(pt_ref, qabs_ref, qpe_ref, knew_ref, wuv_ref, *rest,
                        heads, rows, valid, npages, scale):
    ckv_refs = rest[:npages]
    kpe_refs = rest[npages:2 * npages]
    o_ref = rest[2 * npages]
    qa_scr, qp_scr, m_scr, l_scr, acc_scr = rest[2 * npages + 1:]
    s = pl.program_id(1)
    ns = pl.num_programs(1)
    nkv = qabs_ref.shape[-1]
    c2 = scale * math.log2(math.e)

    @pl.when(s == 0)
    def _():
        qpe = qpe_ref[...]
        for h in range(heads):
            qa_scr[h * rows:(h + 1) * rows, :] = qabs_ref[h]
            qp_scr[h * rows:(h + 1) * rows, :] = qpe[:, h * ROPE_DIM:(h + 1) * ROPE_DIM]
        m_scr[...] = jnp.full(m_scr.shape, -jnp.inf, F32)
        l_scr[...] = jnp.zeros(l_scr.shape, F32)
        acc_scr[...] = jnp.zeros(acc_scr.shape, F32)

    qa = qa_scr[...].astype(BF16)
    qp = qp_scr[...].astype(BF16)

    def update(st, sc, v):
        m_old = m_scr[st]
        m_new = jnp.maximum(m_old, jnp.max(sc, axis=-1, keepdims=True))
        alpha = jnp.exp2((m_old - m_new) * c2)
        p = jnp.exp2((sc - jnp.tile(m_new, (1, sc.shape[1] // LANES))) * c2)
        psum = p[:, :LANES]
        for t in range(1, sc.shape[1] // LANES):
            psum = psum + p[:, t * LANES:(t + 1) * LANES]
        l_scr[st] = alpha * l_scr[st] + psum
        acc_scr[st] = jnp.tile(alpha, (1, nkv // LANES)) * acc_scr[st] + _dot(p.astype(BF16), v)
        m_scr[st] = m_new

    nst = m_scr.shape[0]
    per = npages // nst
    for st in range(nst):
        cs, ps = ckv_refs[st * per:(st + 1) * per], kpe_refs[st * per:(st + 1) * per]
        kb = jnp.concatenate([c_ref[...].astype(BF16) for c_ref in cs], axis=0)
        pt = jnp.concatenate([p_ref[...] for p_ref in ps], axis=1).astype(BF16)
        update(st, _dot_nt(qa, kb) + _dot(qp, pt), kb)

    @pl.when(s == ns - 1)
    def _():
        kn = knew_ref[...]
        kn = jnp.concatenate([kn, jnp.zeros((LANES - rows, kn.shape[1]), kn.dtype)], axis=0).astype(BF16)
        kb = kn[:, :nkv]
        sc = _dot_nt(qa, kb) + _dot_nt(qp, kn[:, nkv:nkv + ROPE_DIM])
        tok = lax.broadcasted_iota(jnp.int32, sc.shape, 0) & (rows - 1)
        key = lax.broadcasted_iota(jnp.int32, sc.shape, 1)
        update(0, jnp.where((key <= tok) & (key < valid), sc, -jnp.inf), kb)
        m = m_scr[0]
        for st in range(1, nst):
            m = jnp.maximum(m, m_scr[st])
        lsum = jnp.zeros(m.shape, F32)
        acc = jnp.zeros(acc_scr.shape[1:], F32)
        for st in range(nst):
            w = jnp.exp2((m_scr[st] - m) * c2)
            lsum = lsum + w * l_scr[st]
            acc = acc + jnp.tile(w, (1, nkv // LANES)) * acc_scr[st]
        ob = (acc / jnp.sum(lsum, axis=-1, keepdims=True)).astype(BF16)
        for p in range(heads // 2):
            lo = _dot(ob, wuv_ref[2 * p])[(2 * p) * rows:(2 * p + 1) * rows]
            hi = _dot(ob, wuv_ref[2 * p + 1])[(2 * p + 1) * rows:(2 * p + 2) * rows]
            o_ref[:, p * LANES:(p + 1) * LANES] = (lo + hi).astype(o_ref.dtype)


def _attn_sample(qabs, qpe, knew, wuv, cache_ckv, cache_kpe, page_table, layer, rows, valid, scale):
    heads, T, R = qabs.shape
    B, n_pages = page_table.shape
    npg = PAGES_PER_STEP
    steps = n_pages // npg
    _, _, page, klat = cache_ckv.shape
    out_w = (heads // 2) * LANES
    kpeT = jnp.swapaxes(cache_kpe, 2, 3)

    def page_spec(shape, p):
        return pl.BlockSpec((None, None) + shape, lambda b, s, pt: (layer, pt[b, s * npg + p], 0, 0))

    grid_spec = pltpu.PrefetchScalarGridSpec(
        num_scalar_prefetch=1,
        grid=(B, steps),
        in_specs=[
            pl.BlockSpec((heads, rows, R), lambda b, s, pt: (0, b, 0)),
            pl.BlockSpec((rows, qpe.shape[1]), lambda b, s, pt: (b, 0)),
            pl.BlockSpec((rows, knew.shape[1]), lambda b, s, pt: (b, 0)),
            wuv.spec(),
        ] + [page_spec((page, klat), p) for p in range(npg)]
        + [page_spec((ROPE_DIM, page), p) for p in range(npg)],
        out_specs=pl.BlockSpec((rows, out_w), lambda b, s, pt: (b, 0)),
        scratch_shapes=[
            pltpu.VMEM((heads * rows, R), F32),
            pltpu.VMEM((heads * rows, ROPE_DIM), F32),
            pltpu.VMEM((SAMPLE_CHAINS, heads * rows, LANES), F32),
            pltpu.VMEM((SAMPLE_CHAINS, heads * rows, LANES), F32),
            pltpu.VMEM((SAMPLE_CHAINS, heads * rows, R), F32),
        ],
    )
    return pl.pallas_call(
        functools.partial(_attn_sample_kernel, heads=heads, rows=rows, valid=valid, npages=npg, scale=scale),
        grid_spec=grid_spec,
        out_shape=jax.ShapeDtypeStruct((T, out_w), F32),
        compiler_params=_cparams(("parallel", "arbitrary")),
        name="attn_sample",
    )(page_table, qabs, qpe, knew, wuv.arr, *([cache_ckv] * npg), *([kpeT] * npg))


def _ssd_kernel(xbc_ref, z_ref, dt_ref, dtT_ref, hist_ref, h0_ref, cw_ref, cb_ref,
                dtb_row_ref, dtb_col_ref, alog_row_ref, alog_col_ref, dskip_ref, gy_ref,
                y_ref, hlast_ref, xp_scr, h_scr, *, l, valid, heads, hdim, nstate, kconv):
    c = pl.program_id(1)
    nc = pl.num_programs(1)
    inner = heads * hdim
    gw = SSM_GROUPS * nstate
    hpg = heads // SSM_GROUPS
    pad = SUBLANES
    lb = xbc_ref.shape[0]

    def pad_rows(a):
        if lb == l:
            return a
        return jnp.concatenate([a, jnp.zeros((l - lb, a.shape[1]), a.dtype)], axis=0)

    @pl.when(c == 0)
    def _():
        xp_scr[0:pad, :] = hist_ref[...]
        h_scr[...] = h0_ref[...]

    xp_scr[pad:pad + l, :] = pad_rows(xbc_ref[...])
    acc = jnp.zeros((l, xbc_ref.shape[1]), F32) + cb_ref[...]
    for k in range(kconv):
        acc = acc + cw_ref[k:k + 1, :] * xp_scr[pl.ds(pad - (kconv - 1) + k, l), :]
    carry = xp_scr[l:l + pad, :]
    xp_scr[0:pad, :] = carry
    xc = _silu(acc)
    xs = xc[:, :inner]
    bm = xc[:, inner:inner + gw]
    cm = xc[:, inner + gw:inner + 2 * gw]

    dt_col = _softplus(pad_rows(dt_ref[...]) + dtb_row_ref[...])
    dt_row = _softplus(dtT_ref[...] + dtb_col_ref[...])
    if valid < l:
        dt_col = jnp.where(lax.broadcasted_iota(jnp.int32, dt_col.shape, 0) < valid, dt_col, 0.0)
        dt_row = jnp.where(lax.broadcasted_iota(jnp.int32, dt_row.shape, 1) < valid, dt_row, 0.0)
    dta_col = dt_col * (-jnp.exp(alog_row_ref[...]))
    dta_row = dt_row * (-jnp.exp(alog_col_ref[...]))

    ri = lax.broadcasted_iota(jnp.int32, (l, l), 0)
    ci = lax.broadcasted_iota(jnp.int32, (l, l), 1)
    causal = ci <= ri
    lower = jnp.where(causal, 1.0, 0.0).astype(BF16)
    upper = jnp.where(ri <= ci, 1.0, 0.0).astype(BF16)
    acs_col = sum(_dot(lower, piece) for piece in _split3(dta_col))
    acs_row = sum(_dot(piece, upper) for piece in _split3(dta_row))
    acs_last = acs_col[l - 1:l, :]
    eacs = jnp.exp(acs_col)
    toend = jnp.exp(acs_last - acs_col)

    lane = lax.broadcasted_iota(jnp.int32, (l, LANES), 1)
    per = LANES // hdim
    srow = lax.broadcasted_iota(jnp.int32, (LANES, nstate), 0)
    ys = []
    for p in range(heads // per):
        g = (p * per) // hpg
        bg = bm[:, g * nstate:(g + 1) * nstate].astype(BF16)
        cg = cm[:, g * nstate:(g + 1) * nstate]
        cbg = _dot_nt(cg.astype(BF16), bg)
        xs_p = xs[:, p * LANES:(p + 1) * LANES]
        dsk_p = dskip_ref[:, p * LANES:(p + 1) * LANES]
        hprev = h_scr[p * LANES:(p + 1) * LANES, :]
        hprev_b = hprev.astype(BF16)
        dt_p = jnp.zeros((l, LANES), F32)
        te_p = jnp.zeros((l, LANES), F32)
        cd_p = jnp.zeros((LANES, nstate), F32)
        for s in range(per):
            h = p * per + s
            sel = (lane // hdim) == s
            dt_p = jnp.where(sel, dt_col[:, h:h + 1], dt_p)
            te_p = jnp.where(sel, toend[:, h:h + 1], te_p)
            cd_p = jnp.where((srow // hdim) == s, jnp.exp(acs_row[h:h + 1, l - 1:l]), cd_p)
        dtx_p = dt_p * xs_p
        y_p = dsk_p * xs_p
        for s in range(per):
            h = p * per + s
            sel = (lane // hdim) == s
            seg = acs_col[:, h:h + 1] - acs_row[h:h + 1, :]
            dec = jnp.exp(jnp.where(causal, seg, -jnp.inf))
            mh = (cbg * dec).astype(BF16)
            y_h = _dot(mh, jnp.where(sel, dtx_p, 0.0).astype(BF16))
            cs = (cg * eacs[:, h:h + 1]).astype(BF16)
            y_h = y_h + jnp.where(sel, _dot_nt(cs, hprev_b), 0.0)
            y_p = y_p + y_h
        st_p = _dot_tn((dtx_p * te_p).astype(BF16), bg)
        h_scr[p * LANES:(p + 1) * LANES, :] = cd_p * hprev + st_p
        ys.append(y_p)
    y = jnp.concatenate(ys, axis=-1)[:lb] * _silu(z_ref[...])
    gsz = inner // SSM_GROUPS
    outs = [_rms(y[:, g * gsz:(g + 1) * gsz]) for g in range(SSM_GROUPS)]
    y_ref[...] = (jnp.concatenate(outs, axis=-1) * gy_ref[...]).astype(y_ref.dtype)

    @pl.when(c == nc - 1)
    def _():
        hlast_ref[...] = h_scr[...]


def _ssd(xbc, z, dt, dtT, hist, h0, wts, B, S, lb, valid, act_dtype):
    T, cdim = xbc.shape
    nc = S // lb
    l = SSD_CHUNK
    assert lb == l or nc == 1
    heads, hdim, nstate = wts["heads"], wts["hdim"], wts["nstate"]
    inner = heads * hdim
    kconv = wts["conv_w"].shape[0]
    row = lambda b, c: (b * nc + c, 0)
    fixed = lambda b, c: (0, 0)
    small = [wts["conv_w"], wts["conv_b"], wts["dtb_row"], wts["dtb_col"], wts["alog_row"], wts["alog_col"],
             wts["dskip"], wts["gy"]]
    return pl.pallas_call(
        functools.partial(_ssd_kernel, l=l, valid=valid, heads=heads, hdim=hdim, nstate=nstate, kconv=kconv),
        grid=(B, nc),
        in_specs=[
            pl.BlockSpec((lb, cdim), row),
            pl.BlockSpec((lb, inner), row),
            pl.BlockSpec((lb, dt.shape[1]), row),
            pl.BlockSpec((None, dtT.shape[1], l), lambda b, c: (b, 0, c)),
            pl.BlockSpec((None, None, SUBLANES, cdim), lambda b, c: (hist[1], b, 0, 0)),
            pl.BlockSpec((None, None, inner, nstate), lambda b, c: (h0[1], b, 0, 0)),
        ] + [a.spec() for a in small],
        out_specs=(pl.BlockSpec((lb, inner), row),
                   pl.BlockSpec((None, inner, nstate), lambda b, c: (b, 0, 0))),
        out_shape=(jax.ShapeDtypeStruct((T, inner), act_dtype),
                   jax.ShapeDtypeStruct((B, inner, nstate), F32)),
        scratch_shapes=[pltpu.VMEM((l + 2 * SUBLANES, cdim), F32),
                        pltpu.VMEM((inner, nstate), F32)],
        compiler_params=_cparams(("parallel", "arbitrary")),
        name="ssd",
    )(xbc, z, dt, dtT, hist[0], h0[0], *[a.arr for a in small])


def _outproj_kernel(a1_ref, a2_ref, w1_ref, w2_ref, x_ref, g_ref, b_ref, o_ref, *, alpha):
    m = _dot(a1_ref[...].astype(BF16), w1_ref[...]) + _dot(a2_ref[...].astype(BF16), w2_ref[...])
    o_ref[...] = _layer_norm(alpha * x_ref[...] + m, g_ref[...], b_ref[...])


def _outproj(a1, a2, w1, w2, x, g, b, alpha, tm):
    T, D = x.shape
    row = lambda i: (i, 0)
    return pl.pallas_call(
        functools.partial(_outproj_kernel, alpha=alpha),
        grid=(T // tm,),
        in_specs=[pl.BlockSpec((tm, a1.shape[1]), row), pl.BlockSpec((tm, a2.shape[1]), row),
                  w1.spec(), w2.spec(), pl.BlockSpec((tm, D), row), g.spec(), b.spec()],
        out_specs=pl.BlockSpec((tm, D), row),
        out_shape=jax.ShapeDtypeStruct((T, D), F32),
        compiler_params=_cparams(("parallel",)),
        name="outproj_ln",
    )(a1, a2, w1.arr, w2.arr, x, g.arr, b.arr)


def _conf_kernel(x_ref, hist_ref, wa_ref, wb_ref, ba_ref, bb_ref, dww_ref, dwb_ref, gn_ref, bn_ref, wo_ref, bo_ref,
                 g_ref, b_ref, o_ref, vlast_ref, xp_scr, *, nseq, rows, halo, ktaps, rblk, alpha):
    c = pl.program_id(1)
    nc = pl.num_programs(1)
    nlb = dww_ref.shape[0]
    first = halo - (ktaps - 1)

    @pl.when(c == 0)
    def _():
        for cb in range(nlb):
            xp_scr[cb, :, 0:halo, :] = hist_ref[:, :, cb * LANES:(cb + 1) * LANES]

    x = x_ref[...]
    xb = x.astype(BF16)
    per = 2
    convs = []
    for cc in range(nlb // per):
        sl = slice(cc * per * LANES, (cc + 1) * per * LANES)
        ua = _dot(xb, wa_ref[:, sl]) + ba_ref[:, sl]
        ub = _dot(xb, wb_ref[:, sl]) + bb_ref[:, sl]
        v = ua * jax.nn.sigmoid(ub)
        for h in range(per):
            cb = cc * per + h
            xp_scr[cb, :, halo:halo + rows, :] = v[:, h * LANES:(h + 1) * LANES].reshape(nseq, rows, LANES)
            outs = []
            for s in range(nseq):
                for rb in range(rows // rblk):
                    acc = jnp.zeros((rblk, LANES), F32) + dwb_ref[cb]
                    for k in range(ktaps):
                        acc = acc + dww_ref[cb, k:k + 1, :] * xp_scr[cb, s, pl.ds(rb * rblk + first + k, rblk), :]
                    outs.append(acc)
            convs.append(jnp.concatenate(outs, axis=0))
            tail = xp_scr[cb, :, rows:rows + halo, :]
            xp_scr[cb, :, 0:halo, :] = tail

    @pl.when(c == nc - 1)
    def _():
        for cb in range(nlb):
            vlast_ref[:, :, cb * LANES:(cb + 1) * LANES] = xp_scr[cb, :, halo:halo + rows, :]

    conv = jnp.concatenate(convs, axis=-1)
    cn = _silu(_layer_norm(conv, gn_ref[...], bn_ref[...]))
    m = _dot(cn.astype(BF16), wo_ref[...]) + bo_ref[...]
    o_ref[...] = _layer_norm(alpha * x + m, g_ref[...], b_ref[...])


def _conf(x, hist, wts, g, b, B, S, nseq, rows, alpha):
    T, D = x.shape
    nc = S // rows
    assert nseq == 1 or nc == 1
    halo = hist[0].shape[2]
    nlb, ktaps, _ = wts["dww"].shape
    C = nlb * LANES
    tmr = nseq * rows
    row = lambda bb, c: (bb * nc + c, 0)
    small = [wts[k] for k in ("wa", "wb", "ba", "bb", "dww", "dwb", "gn", "bn", "wo", "bo")] + [g, b]
    return pl.pallas_call(
        functools.partial(_conf_kernel, nseq=nseq, rows=rows, halo=halo, ktaps=ktaps, rblk=min(rows, 64),
                          alpha=alpha),
        grid=(B // nseq, nc),
        in_specs=[pl.BlockSpec((tmr, D), row),
                  pl.BlockSpec((None, nseq, halo, C), lambda bb, c: (hist[1], bb, 0, 0))]
        + [a.spec() for a in small],
        out_specs=(pl.BlockSpec((tmr, D), row), pl.BlockSpec((nseq, rows, C), lambda bb, c: (bb, 0, 0))),
        out_shape=(jax.ShapeDtypeStruct((T, D), F32), jax.ShapeDtypeStruct((B, rows, C), F32)),
        scratch_shapes=[pltpu.VMEM((nlb, nseq, rows + halo, LANES), F32)],
        compiler_params=_cparams(("parallel", "arbitrary")),
        name="conformer",
    )(x, hist[0], *[a.arr for a in small])


def _router_kernel(x_ref, wh_ref, wl_ref, rb_ref, g_ref, *, n_exp):
    x = x_ref[...]
    xh = x.astype(BF16)
    xl = (x - xh.astype(F32)).astype(BF16)
    logits = _dot_nt(wh_ref[...], xh) + _dot_nt(wl_ref[...], xh) + _dot_nt(wh_ref[...], xl)
    probs = jax.nn.sigmoid(logits)
    sel = probs + rb_ref[...]
    per = n_exp // N_GROUPS
    s = [sel[e:e + 1, :] for e in range(n_exp)]
    top = []
    for e in range(n_exp):
        g0 = (e // per) * per
        rank = jnp.zeros(s[e].shape, F32)
        for j in range(g0, g0 + per):
            if j < e:
                rank = rank + jnp.where(s[j] >= s[e], 1.0, 0.0)
            elif j > e:
                rank = rank + jnp.where(s[j] > s[e], 1.0, 0.0)
        top.append(rank < TOP_K)
    grp = []
    for g in range(N_GROUPS):
        tot = jnp.zeros(s[0].shape, F32)
        for e in range(g * per, (g + 1) * per):
            tot = tot + jnp.where(top[e], s[e], 0.0)
        grp.append(tot)
    chosen = []
    for g in range(N_GROUPS):
        ok = None
        for j in range(N_GROUPS):
            if j == g:
                continue
            t = (grp[j] < grp[g]) if j < g else (grp[j] <= grp[g])
            ok = t if ok is None else (ok & t)
        chosen.append(ok)
    w = [jnp.where(chosen[e // per] & top[e], probs[e:e + 1, :], 0.0) for e in range(n_exp)]
    den = w[0]
    for e in range(1, n_exp):
        den = den + w[e]
    for e in range(n_exp):
        g_ref[e:e + 1, :] = w[e] / den


def _router(x, wh, wl, rb, tm):
    T, D = x.shape
    E = wh.shape[0]
    return pl.pallas_call(
        functools.partial(_router_kernel, n_exp=E),
        grid=(T // tm,),
        in_specs=[pl.BlockSpec((tm, D), lambda i: (i, 0)), wh.spec(), wl.spec(), rb.spec()],
        out_specs=pl.BlockSpec((E, tm), lambda i: (0, i)),
        out_shape=jax.ShapeDtypeStruct((E, T), F32),
        compiler_params=_cparams(("parallel",)),
        name="router",
    )(x, wh.arr, wl.arr, rb.arr)


def _moe_kernel(x_ref, gates_ref, wg_ref, wu_ref, wd_ref, g_ref, b_ref, o_ref, xb_scr, acc_scr, *, alpha):
    grp = pl.program_id(1)
    ngrp = pl.num_programs(1)
    per, F, D = wd_ref.shape

    @pl.when(grp == 0)
    def _():
        xb_scr[...] = x_ref[...].astype(BF16)
        acc_scr[...] = jnp.zeros(acc_scr.shape, F32)

    xb = xb_scr[...]
    gates = gates_ref[...]
    lane = lax.broadcasted_iota(jnp.int32, gates.shape, 1)
    hs = []
    for j in range(per):
        gate = jnp.sum(jnp.where(lane == grp * per + j, gates, 0.0), axis=-1, keepdims=True)
        h = _silu(_dot(xb, wg_ref[j])) * _dot(xb, wu_ref[j])
        hs.append((h * gate).astype(BF16))
    acc_scr[...] += _dot(jnp.concatenate(hs, axis=-1), wd_ref[...].reshape(per * F, D))

    @pl.when(grp == ngrp - 1)
    def _():
        o_ref[...] = _layer_norm(alpha * x_ref[...] + acc_scr[...], g_ref[...], b_ref[...])


def _moe(x, gates, wg, wu, wd, g, b, layer, alpha, tm):
    T, D = x.shape
    _, E, _, F = wg.shape
    per = E // N_GROUPS
    row = lambda i, e: (i, 0)
    fixed = lambda i, e: (0, 0)
    return pl.pallas_call(
        functools.partial(_moe_kernel, alpha=alpha),
        grid=(T // tm, N_GROUPS),
        in_specs=[pl.BlockSpec((tm, D), row), pl.BlockSpec((tm, E), row),
                  pl.BlockSpec((None, per, D, F), lambda i, e: (layer, e, 0, 0)),
                  pl.BlockSpec((None, per, D, F), lambda i, e: (layer, e, 0, 0)),
                  pl.BlockSpec((None, per, F, D), lambda i, e: (layer, e, 0, 0)),
                  g.spec(), b.spec()],
        out_specs=pl.BlockSpec((tm, D), row),
        out_shape=jax.ShapeDtypeStruct((T, D), F32),
        scratch_shapes=[pltpu.VMEM((tm, D), BF16), pltpu.VMEM((tm, D), F32)],
        compiler_params=_cparams(("parallel", "arbitrary")),
        name="moe",
    )(x, gates, wg, wu, wd, g.arr, b.arr)


def _rope_tables(pos, heads, reps):
    half = ROPE_DIM // 2
    inv = ROPE_BASE ** (-jnp.arange(half, dtype=F32) / half)
    ang = pos.astype(F32)[:, None] * inv[None, :]
    cos, sin = jnp.cos(ang), jnp.sin(ang)
    cos_t = jnp.tile(jnp.concatenate([cos, cos], -1), (reps, heads))
    sin_t = jnp.tile(jnp.concatenate([-sin, sin], -1), (reps, heads))
    return cos_t, sin_t


def _swap_halves(w):
    half = ROPE_DIM // 2
    return jnp.concatenate([w[..., half:], w[..., :half]], axis=-1)


def _prep_ab(w_in, g_q, g_kv, w_uq, w_uk, w_uv, conv_w, conv_b, dt_bias, a_log, d_skip, g_y, w_out):
    n_a, q_lora = g_q.shape
    kv_lora = g_kv.shape[1]
    heads = w_uq.shape[2]
    s_heads = dt_bias.shape[1]
    cdim = conv_w.shape[2]
    o1 = q_lora
    o2 = o1 + kv_lora
    o3 = o2 + ROPE_DIM
    inner = w_in.shape[2] - o3 - cdim - s_heads
    o4 = o3 + inner
    o5 = o4 + cdim
    hdim = inner // s_heads
    nstate = (cdim - inner) // (2 * SSM_GROUPS)
    wkp = w_in[:, :, o2:o3]
    dt_pad = jnp.pad(w_in[:, :, o5:], ((0, 0), (0, 0), (0, LANES - s_heads)))
    pieces = [("q", w_in[:, :, :o1]), ("kv", w_in[:, :, o1:o2]), ("kpa", jnp.tile(wkp, (1, 1, heads))),
              ("kpb", jnp.tile(_swap_halves(wkp), (1, 1, heads))), ("z", w_in[:, :, o3:o4]),
              ("xbc", w_in[:, :, o4:o5]), ("dt", dt_pad)]
    cols, off = {}, 0
    for name, p in pieces:
        cols[name] = (off, off + p.shape[2])
        off += p.shape[2]
    w_all = jnp.concatenate([p for _, p in pieces], axis=2).astype(BF16)

    wn = w_uq[..., :NOPE_DIM].reshape(n_a, q_lora, heads * NOPE_DIM).astype(BF16)
    wpe = w_uq[..., NOPE_DIM:]
    wpa = wpe.reshape(n_a, q_lora, heads * ROPE_DIM).astype(BF16)
    wpb = _swap_halves(wpe).reshape(n_a, q_lora, heads * ROPE_DIM).astype(BF16)
    per = LANES // NOPE_DIM
    slot = (jnp.arange(heads)[:, None] % per == jnp.arange(per)[None, :]).astype(F32)
    ukt = jnp.transpose(w_uk, (0, 2, 3, 1))
    wuk = (ukt[:, :, None] * slot[None, :, :, None, None]).reshape(n_a, heads, LANES, kv_lora).astype(BF16)
    v_dim = w_uv.shape[3]
    vper = LANES // v_dim
    vslot = (jnp.arange(heads)[:, None] % vper == jnp.arange(vper)[None, :]).astype(F32)
    uvt = jnp.transpose(w_uv, (0, 2, 1, 3))
    wuv = (uvt[:, :, :, None] * vslot[None, :, None, :, None]).reshape(n_a, heads, kv_lora, LANES).astype(BF16)

    def row(v, width):
        return jnp.pad(v, ((0, 0), (0, width - v.shape[1])))[:, None, :]

    def col(v, height):
        return jnp.pad(v, ((0, 0), (0, height - v.shape[1])))[:, :, None]

    ssd = dict(conv_w=conv_w, conv_b=conv_b[:, None, :],
               dtb_row=row(dt_bias, LANES), dtb_col=col(dt_bias, 2 * SUBLANES),
               alog_row=row(a_log, LANES), alog_col=col(a_log, 2 * SUBLANES),
               dskip=jnp.repeat(d_skip, hdim, axis=1)[:, None, :], gy=g_y[:, None, :])
    return dict(cols=cols, w_all=w_all, gq=g_q[:, None, :], gkv=g_kv[:, None, :], wn=wn, wpa=wpa, wpb=wpb,
                wuk=wuk, wuv=wuv, ssd=ssd, w_out=w_out.astype(BF16), n_mla=heads * v_dim,
                dims=dict(heads=s_heads, hdim=hdim, nstate=nstate))


def _ab_layer(st, i):
    lay = lambda name: _layer(st[name], i)
    n_out = st["w_out"].shape[1]
    assert 2 * st["n_mla"] == n_out
    ssd = dict(st["dims"], **{k: _layer(v, i) for k, v in st["ssd"].items()})
    return dict(cols=st["cols"], w_all=lay("w_all"), gq=lay("gq"), gkv=lay("gkv"), wn=lay("wn"), wpa=lay("wpa"),
                wpb=lay("wpb"), wuk=lay("wuk"), wuv=lay("wuv"), ssd=ssd,
                w_out_a=_layer(st["w_out"], i, axis=1, part=0, nparts=2),
                w_out_b=_layer(st["w_out"], i, axis=1, part=1, nparts=2))


def _mixer_ab(x, grp, wts, tabs, ln_g, ln_b, alpha, paged=None):
    B, S, tm, l, valid, act = grp["B"], grp["S"], grp["tm"], grp["l"], grp["valid"], grp["act"]
    cos_t, sin_t = tabs
    hq, ckv, kpe, kcat, z, xbc, dt = _inproj(x, wts["w_all"], wts["gq"], wts["gkv"], cos_t, sin_t,
                                            wts["cols"], tm, act)
    qabs, qpe = _qprep(hq, wts["wn"], wts["wpa"], wts["wpb"], wts["wuk"], cos_t, sin_t, tm, act)
    scale = (NOPE_DIM + ROPE_DIM) ** -0.5
    if paged is None:
        o_mla = _attn_prompt(qabs, qpe, kcat, wts["wuv"], B, S, grp["tq"], grp["tk"], scale, act)
    else:
        cache_ckv, cache_kpe, page_table, idx = paged
        o_mla = _attn_sample(qabs, qpe, kcat, wts["wuv"], cache_ckv, cache_kpe, page_table, idx, S, valid, scale)
    s_heads = wts["ssd"]["heads"]
    dtT = jnp.swapaxes(dt[:, :s_heads].reshape(B, S, s_heads), 1, 2)
    dtT = jnp.pad(dtT, ((0, 0), (0, 2 * SUBLANES - s_heads), (0, max(SSD_CHUNK - S, 0))))
    y, h_last = _ssd(xbc, z, dt, dtT, grp["ssm_hist"], grp["ssm_h0"], wts["ssd"], B, S, l, valid, act)
    x_new = _outproj(o_mla, y, wts["w_out_a"], wts["w_out_b"], x, ln_g, ln_b, alpha, tm)
    return x_new, ckv, kpe, h_last, xbc


def _mixer_c(x, grp, wts, ln_g, ln_b, alpha):
    return _conf(x, grp["conf_hist"], wts, ln_g, ln_b, grp["B"], grp["S"], grp["conf_nseq"], grp["conf_rows"], alpha)


def _moe_block(x, rt, wg, wu, wd, ln_g, ln_b, layer, alpha, tm_r, tm_m):
    gates_t = _router(x, rt["wh"], rt["wl"], rt["rb"], tm_r)
    return _moe(x, gates_t.T, wg, wu, wd, ln_g, ln_b, layer, alpha, tm_m)


def kernel(x_prompt, x_sample, cache_ckv, cache_kpe, state_ssm, state_ssm_conv, state_conf_conv, page_table,
           w_in_ab, g_q_norm, g_kv_norm, w_uq, w_uk, w_uv, ssm_conv_w, ssm_conv_b, ssm_dt_bias, ssm_a_log,
           ssm_d, ssm_norm_g, w_out_ab, conf_w_in, conf_b_in, conf_dw_w, conf_dw_b, conf_norm_g, conf_norm_b,
           conf_w_out, conf_b_out, ln_mix_g, ln_mix_b, ln_ffn_g, ln_ffn_b, w_router, router_bias,
           moe_w_gate, moe_w_up, moe_w_down):
    bp, S, D = x_prompt.shape
    bd, t_new, _ = x_sample.shape
    depth = ln_mix_g.shape[0]
    past_len = page_table.shape[1] * PAGE_SIZE
    alpha = (2 * depth) ** 0.25
    heads = w_uq.shape[2]
    n_a = w_in_ab.shape[0]
    n_c = conf_w_in.shape[0]
    conf_ch = conf_dw_w.shape[2]
    conf_k = conf_dw_w.shape[1]
    kconv = ssm_conv_w.shape[1]
    cdim = ssm_conv_w.shape[2]
    rows = SAMPLE_ROWS
    tm_p = 256

    ab_st = _prep_ab(w_in_ab, g_q_norm, g_kv_norm, w_uq, w_uk, w_uv, ssm_conv_w, ssm_conv_b, ssm_dt_bias,
                     ssm_a_log, ssm_d, ssm_norm_g, w_out_ab)
    ab = [_ab_layer(ab_st, i) for i in range(n_a)]
    nlb = conf_ch // LANES
    cw_in = conf_w_in.astype(BF16)
    cb_in = conf_b_in[:, None, :]
    cw_dw = jnp.transpose(conf_dw_w.reshape(n_c, conf_k, nlb, LANES), (0, 2, 1, 3))
    cb_dw = conf_dw_b.reshape(n_c, nlb, 1, LANES)
    cw_out = conf_w_out.astype(BF16)
    c_gn, c_bn, cb_out = conf_norm_g[:, None, :], conf_norm_b[:, None, :], conf_b_out[:, None, :]
    cw = [dict(wa=_layer(cw_in, i, axis=2, part=0, nparts=2), wb=_layer(cw_in, i, axis=2, part=1, nparts=2),
               ba=_layer(cb_in, i, axis=2, part=0, nparts=2), bb=_layer(cb_in, i, axis=2, part=1, nparts=2),
               dww=_layer(cw_dw, i), dwb=_layer(cb_dw, i), gn=_layer(c_gn, i), bn=_layer(c_bn, i),
               wo=_layer(cw_out, i), bo=_layer(cb_out, i))
          for i in range(n_c)]
    wr_t = w_router.T
    wr_h = wr_t.astype(BF16)
    rt = dict(wh=_W(wr_h), wl=_W((wr_t - wr_h.astype(F32)).astype(BF16)), rb=_W(router_bias[:, None]))
    wg_b, wu_b, wd_b = moe_w_gate.astype(BF16), moe_w_up.astype(BF16), moe_w_down.astype(BF16)
    ln_mg, ln_mb = ln_mix_g[:, None, :], ln_mix_b[:, None, :]
    ln_fg, ln_fb = ln_ffn_g[:, None, :], ln_ffn_b[:, None, :]

    halo = 4 * SUBLANES
    sub_heads = ssm_dt_bias.shape[1]
    inner = state_ssm.shape[2] * state_ssm.shape[3]
    nstate = state_ssm.shape[4]
    gp = dict(B=bp, S=S, tm=tm_p, tq=256, tk=512, l=min(SSD_CHUNK, S), valid=min(SSD_CHUNK, S), act=BF16,
              ssm_hist=(jnp.zeros((1, bp, SUBLANES, cdim), F32), 0),
              ssm_h0=(jnp.zeros((1, bp, inner, nstate), F32), 0),
              conf_hist=(jnp.zeros((1, bp, halo, conf_ch), F32), 0), conf_nseq=1, conf_rows=min(tm_p, S))
    ssm_hist_s = jnp.pad(state_ssm_conv, ((0, 0), (0, 0), (SUBLANES - (kconv - 1), 0), (0, 0)))
    ssm_h0_s = state_ssm.reshape(n_a, bd, inner, nstate)
    conf_hist_s = jnp.pad(state_conf_conv, ((0, 0), (0, 0), (halo - (conf_k - 1), 0), (0, 0)))
    gs = dict(B=bd, S=rows, tm=bd * rows, l=rows, valid=t_new, act=F32, conf_nseq=bd, conf_rows=rows)
    tabs_p = _rope_tables(jnp.arange(S, dtype=jnp.int32), heads, 1)
    pos_s = past_len + jnp.arange(rows, dtype=jnp.int32)
    tabs_s = _rope_tables(pos_s, heads, bd)

    hp = x_prompt.reshape(bp * S, D)
    hs = jnp.pad(x_sample, ((0, 0), (0, rows - t_new), (0, 0))).reshape(bd * rows, D)
    out_p = dict(ckv=[], kpe=[], ssm=[], sconv=[], cconv=[])
    out_s = dict(ckv=[], kpe=[], ssm=[], sconv=[], cconv=[])
    for layer in range(depth):
        i = layer // 2
        lg, lb = _layer(ln_mg, layer), _layer(ln_mb, layer)
        if layer % 2 == 0:
            hp, c1, k1, h1, xbc1 = _mixer_ab(hp, gp, ab[i], tabs_p, lg, lb, alpha)
            gs_l = dict(gs, ssm_hist=(ssm_hist_s, i), ssm_h0=(ssm_h0_s, i))
            hs, c2, k2, h2, xbc2 = _mixer_ab(hs, gs_l, ab[i], tabs_s, lg, lb, alpha,
                                             paged=(cache_ckv, cache_kpe, page_table, i))
            out_p["ckv"].append(c1.reshape(bp, S, -1))
            out_p["kpe"].append(k1.reshape(bp, S, -1)[..., :ROPE_DIM])
            out_p["ssm"].append(h1.reshape(bp, sub_heads, -1, nstate))
            out_p["sconv"].append(xbc1.reshape(bp, S, cdim)[:, S - (kconv - 1):])
            out_s["ckv"].append(c2.reshape(bd, rows, -1)[:, :t_new])
            out_s["kpe"].append(k2.reshape(bd, rows, -1)[:, :t_new, :ROPE_DIM])
            out_s["ssm"].append(h2.reshape(bd, sub_heads, -1, nstate))
            xp = jnp.concatenate([state_ssm_conv[i], xbc2.reshape(bd, rows, cdim)[:, :t_new]], axis=1)
            out_s["sconv"].append(xp[:, xp.shape[1] - (kconv - 1):])
        else:
            hp, v1 = _mixer_c(hp, gp, cw[i], lg, lb, alpha)
            gs_l = dict(gs, conf_hist=(conf_hist_s, i))
            hs, v2 = _mixer_c(hs, gs_l, cw[i], lg, lb, alpha)
            out_p["cconv"].append(v1[:, v1.shape[1] - (conf_k - 1):])
            vp = jnp.concatenate([state_conf_conv[i], v2[:, :t_new]], axis=1)
            out_s["cconv"].append(vp[:, vp.shape[1] - (conf_k - 1):])
        fg, fb = _layer(ln_fg, layer), _layer(ln_fb, layer)
        hp = _moe_block(hp, rt, wg_b, wu_b, wd_b, fg, fb, layer, alpha, 512, 1024)
        hs = _moe_block(hs, rt, wg_b, wu_b, wd_b, fg, fb, layer, alpha, bd * rows, bd * rows)
    y_p = hp.reshape(bp, S, D)
    y_s = hs.reshape(bd, rows, D)[:, :t_new]
    return (y_p, y_s,
            jnp.stack(out_p["ckv"]), jnp.stack(out_p["kpe"]), jnp.stack(out_p["ssm"]),
            jnp.stack(out_p["sconv"]), jnp.stack(out_p["cconv"]),
            jnp.stack(out_s["ckv"]), jnp.stack(out_s["kpe"]), jnp.stack(out_s["ssm"]),
            jnp.stack(out_s["sconv"]), jnp.stack(out_s["cconv"]))
```

```python
import functools
import math

import jax
import jax.numpy as jnp
from jax import lax
from jax.experimental import pallas as pl
from jax.experimental.pallas import tpu as pltpu

F32 = jnp.float32
BF16 = jnp.bfloat16

PAGE_SIZE = 128
NOPE_DIM = 64
ROPE_DIM = 32
ROPE_BASE = 10000.0
SSM_GROUPS = 2
SSD_CHUNK = 128
N_GROUPS = 4
TOP_K = 2
EPS = 1e-6

LANES = 128
SUBLANES = 8
VMEM_LIMIT_BYTES = 56 * 1024 * 1024
SAMPLE_ROWS = SUBLANES
PAGES_PER_STEP = 16
SAMPLE_CHAINS = 2


def _dot(a, b):
    return jnp.dot(a, b, preferred_element_type=F32)


def _dot_nt(a, b):
    return lax.dot_general(a, b, (((1,), (1,)), ((), ())), preferred_element_type=F32)


def _dot_tn(a, b):
    return lax.dot_general(a, b, (((0,), (0,)), ((), ())), preferred_element_type=F32)


def _rms(x):
    return x * lax.rsqrt(jnp.mean(x * x, axis=-1, keepdims=True) + EPS)


def _layer_norm(r, g, b):
    mu = jnp.mean(r, axis=-1, keepdims=True)
    d = r - mu
    var = jnp.mean(d * d, axis=-1, keepdims=True)
    return d * lax.rsqrt(var + EPS) * g + b


def _silu(x):
    return x * jax.nn.sigmoid(x)


def _softplus(x):
    return jnp.maximum(x, 0.0) + jnp.log1p(jnp.exp(-jnp.abs(x)))


def _split3(x):
    hi = x.astype(BF16)
    r1 = x - hi.astype(F32)
    mid = r1.astype(BF16)
    lo = (r1 - mid.astype(F32)).astype(BF16)
    return hi, mid, lo


def _cparams(sem):
    return pltpu.CompilerParams(dimension_semantics=sem, vmem_limit_bytes=VMEM_LIMIT_BYTES)


class _W:
    def __init__(self, arr, block=None, index=None):
        self.arr = arr
        self.block = tuple(arr.shape) if block is None else tuple(block)
        self.index = (0,) * arr.ndim if index is None else tuple(index)
        self.shape = tuple(b for b in self.block if b is not None)

    def spec(self):
        index = self.index
        return pl.BlockSpec(self.block, lambda *_: index)


def _layer(arr, i, axis=None, part=0, nparts=1):
    block = [None] + list(arr.shape[1:])
    index = [i] + [0] * (arr.ndim - 1)
    if axis is not None:
        block[axis] = arr.shape[axis] // nparts
        index[axis] = part
    return _W(arr, block, index)


def _inproj_kernel(x_ref, w_ref, gq_ref, gkv_ref, cos_ref, sin_ref,
                   hq_ref, ckv_ref, kpe_ref, kcat_ref, z_ref, xbc_ref, dt_ref, *, cols):
    xb = x_ref[...].astype(BF16)

    def proj(name):
        a, b = cols[name]
        return _dot(xb, w_ref[:, a:b])

    hq_ref[...] = (_rms(proj("q")) * gq_ref[...]).astype(hq_ref.dtype)
    ckv = _rms(proj("kv")) * gkv_ref[...]
    ckv_ref[...] = ckv
    kpe = proj("kpa") * cos_ref[...] + proj("kpb") * sin_ref[...]
    kpe_ref[...] = kpe
    nkv = ckv.shape[-1]
    kcat_ref[:, :nkv] = ckv.astype(kcat_ref.dtype)
    kcat_ref[:, nkv:] = kpe.astype(kcat_ref.dtype)
    z_ref[...] = proj("z")
    xbc_ref[...] = proj("xbc")
    dt_ref[...] = proj("dt")


def _inproj(x, w, gq, gkv, cos_t, sin_t, cols, tm, act_dtype):
    T, D = x.shape
    n_tab = cos_t.shape[0] // tm
    widths = {k: b - a for k, (a, b) in cols.items()}
    row = lambda i: (i, 0)
    tab = lambda i: (i % n_tab, 0)
    out_shapes = (
        jax.ShapeDtypeStruct((T, widths["q"]), act_dtype),
        jax.ShapeDtypeStruct((T, widths["kv"]), F32),
        jax.ShapeDtypeStruct((T, widths["kpa"]), F32),
        jax.ShapeDtypeStruct((T, widths["kv"] + widths["kpa"]), act_dtype),
        jax.ShapeDtypeStruct((T, widths["z"]), F32),
        jax.ShapeDtypeStruct((T, widths["xbc"]), F32),
        jax.ShapeDtypeStruct((T, widths["dt"]), F32),
    )
    return pl.pallas_call(
        functools.partial(_inproj_kernel, cols=cols),
        grid=(T // tm,),
        in_specs=[
            pl.BlockSpec((tm, D), row),
            w.spec(), gq.spec(), gkv.spec(),
            pl.BlockSpec((tm, cos_t.shape[1]), tab),
            pl.BlockSpec((tm, sin_t.shape[1]), tab),
        ],
        out_specs=tuple(pl.BlockSpec((tm, s.shape[1]), row) for s in out_shapes),
        out_shape=out_shapes,
        compiler_params=_cparams(("parallel",)),
        name="inproj",
    )(x, w.arr, gq.arr, gkv.arr, cos_t, sin_t)


def _qprep_kernel(hq_ref, wn_ref, wpa_ref, wpb_ref, wuk_ref, cos_ref, sin_ref, qabs_ref, qpe_ref, *, heads):
    hq = hq_ref[...].astype(BF16)
    qn = _dot(hq, wn_ref[...])
    qpe = _dot(hq, wpa_ref[...]) * cos_ref[...] + _dot(hq, wpb_ref[...]) * sin_ref[...]
    qpe_ref[...] = qpe.astype(qpe_ref.dtype)
    per = LANES // NOPE_DIM
    for h in range(heads):
        slab = qn[:, (h // per) * LANES:(h // per + 1) * LANES].astype(BF16)
        qabs_ref[h] = _dot(slab, wuk_ref[h]).astype(qabs_ref.dtype)


def _qprep(hq, wn, wpa, wpb, wuk, cos_t, sin_t, tm, act_dtype):
    T, Q = hq.shape
    heads, _, R = wuk.shape
    n_tab = cos_t.shape[0] // tm
    row = lambda i: (i, 0)
    tab = lambda i: (i % n_tab, 0)
    return pl.pallas_call(
        functools.partial(_qprep_kernel, heads=heads),
        grid=(T // tm,),
        in_specs=[
            pl.BlockSpec((tm, Q), row),
            wn.spec(), wpa.spec(), wpb.spec(), wuk.spec(),
            pl.BlockSpec((tm, cos_t.shape[1]), tab),
            pl.BlockSpec((tm, sin_t.shape[1]), tab),
        ],
        out_specs=(pl.BlockSpec((heads, tm, R), lambda i: (0, i, 0)),
                   pl.BlockSpec((tm, wpa.shape[1]), row)),
        out_shape=(jax.ShapeDtypeStruct((heads, T, R), act_dtype),
                   jax.ShapeDtypeStruct((T, wpa.shape[1]), act_dtype)),
        compiler_params=_cparams(("parallel",)),
        name="qprep",
    )(hq, wn.arr, wpa.arr, wpb.arr, wuk.arr, cos_t, sin_t)


def _attn_kernel(qi_ref, kj_ref, qabs_ref, qpe_ref, k_ref, wuv_ref, o_ref,
                 q_scr, m_scr, l_scr, acc_scr, *, heads, tq, tk, scale):
    s = pl.program_id(1)
    i = qi_ref[s]
    j = kj_ref[s]
    nkv = qabs_ref.shape[-1]
    c2 = scale * math.log2(math.e)

    @pl.when(j == 0)
    def _():
        qpe = qpe_ref[...]
        lane = lax.broadcasted_iota(jnp.int32, qpe.shape, 1)
        for h in range(heads):
            q_scr[h * tq:(h + 1) * tq, :nkv] = qabs_ref[h].astype(BF16)
            q_scr[h * tq:(h + 1) * tq, nkv:] = jnp.where(lane // ROPE_DIM == h, qpe, 0).astype(BF16)
        m_scr[...] = jnp.full(m_scr.shape, -jnp.inf, F32)
        l_scr[...] = jnp.zeros(l_scr.shape, F32)
        acc_scr[...] = jnp.zeros(acc_scr.shape, F32)

    k = k_ref[...]

    def update(sc):
        m_old = m_scr[...]
        m_new = jnp.maximum(m_old, jnp.max(sc, axis=-1, keepdims=True))
        alpha = jnp.exp2((m_old - m_new) * c2)
        p = jnp.exp2((sc - jnp.tile(m_new, (1, tk // LANES))) * c2)
        psum = p[:, :LANES]
        for t in range(1, tk // LANES):
            psum = psum + p[:, t * LANES:(t + 1) * LANES]
        l_scr[...] = alpha * l_scr[...] + psum
        acc_scr[...] = jnp.tile(alpha, (1, nkv // LANES)) * acc_scr[...] + _dot(p.astype(BF16), k[:, :nkv])
        m_scr[...] = m_new

    diag = ((i + 1) * tq - 1) // tk

    @pl.when(j < diag)
    def _():
        update(_dot_nt(q_scr[...], k))

    @pl.when(j == diag)
    def _():
        sc = _dot_nt(q_scr[...], k)
        r = lax.broadcasted_iota(jnp.int32, sc.shape, 0)
        c = lax.broadcasted_iota(jnp.int32, sc.shape, 1)
        update(jnp.where(j * tk + c <= i * tq + (r & (tq - 1)), sc, -jnp.inf))
        o = acc_scr[...] / jnp.sum(l_scr[...], axis=-1, keepdims=True)
        for p in range(heads // 2):
            lo = o[(2 * p) * tq:(2 * p + 1) * tq].astype(BF16)
            hi = o[(2 * p + 1) * tq:(2 * p + 2) * tq].astype(BF16)
            o_ref[:, p * LANES:(p + 1) * LANES] = (
                _dot(lo, wuv_ref[2 * p]) + _dot(hi, wuv_ref[2 * p + 1])).astype(o_ref.dtype)


def _attn_prompt(qabs, qpe, kcat, wuv, B, S, tq, tk, scale, act_dtype):
    heads, T, R = qabs.shape
    assert tk % tq == 0 and S % tk == 0
    nq = S // tq
    nk = S // tk
    pairs = [(i, j) for i in range(nq) for j in range(((i + 1) * tq - 1) // tk + 1)]
    qi = jnp.asarray([p[0] for p in pairs], jnp.int32)
    kj = jnp.asarray([p[1] for p in pairs], jnp.int32)
    kw = kcat.shape[1]
    out_w = (heads // 2) * LANES
    grid_spec = pltpu.PrefetchScalarGridSpec(
        num_scalar_prefetch=2,
        grid=(B, len(pairs)),
        in_specs=[
            pl.BlockSpec((heads, tq, R), lambda b, s, qi, kj: (0, b * nq + qi[s], 0)),
            pl.BlockSpec((tq, qpe.shape[1]), lambda b, s, qi, kj: (b * nq + qi[s], 0)),
            pl.BlockSpec((tk, kw), lambda b, s, qi, kj: (b * nk + kj[s], 0)),
            wuv.spec(),
        ],
        out_specs=pl.BlockSpec((tq, out_w), lambda b, s, qi, kj: (b * nq + qi[s], 0)),
        scratch_shapes=[
            pltpu.VMEM((heads * tq, kw), BF16),
            pltpu.VMEM((heads * tq, LANES), F32),
            pltpu.VMEM((heads * tq, LANES), F32),
            pltpu.VMEM((heads * tq, R), F32),
        ],
    )
    return pl.pallas_call(
        functools.partial(_attn_kernel, heads=heads, tq=tq, tk=tk, scale=scale),
        grid_spec=grid_spec,
        out_shape=jax.ShapeDtypeStruct((T, out_w), act_dtype),
        compiler_params=_cparams(("parallel", "arbitrary")),
        name="attn_prompt",
    )(qi, kj, qabs, qpe, kcat, wuv.arr)


def _attn_sample_kernel(pt_ref, qabs_ref, qpe_ref, knew_ref, wuv_ref, ckv_hbm, kpe_hbm, o_ref,
                        qa_scr, qp_scr, m_scr, l_scr, acc_scr, kbuf, pbuf, sem,
                        *, heads, rows, valid, npages, scale, layer):
    b = pl.program_id(0)
    s = pl.program_id(1)
    nb = pl.num_programs(0)
    ns = pl.num_programs(1)
    nkv = qabs_ref.shape[-1]
    page = kbuf.shape[1] // npages
    c2 = scale * math.log2(math.e)
    g = b * ns + s
    slot = g % 2

    def page_copies(bb, ss, sl):
        out = []
        for p in range(npages):
            pg = pt_ref[bb, ss * npages + p]
            out.append(pltpu.make_async_copy(ckv_hbm.at[layer, pg], kbuf.at[sl, pl.ds(p * page, page), :],
                                             sem.at[0, sl]))
            out.append(pltpu.make_async_copy(kpe_hbm.at[layer, pg], pbuf.at[sl, :, pl.ds(p * page, page)],
                                             sem.at[1, sl]))
        return out

    @pl.when(g == 0)
    def _():
        for cp in page_copies(b, s, slot):
            cp.start()

    @pl.when(g + 1 < nb * ns)
    def _():
        gn = g + 1
        for cp in page_copies(gn // ns, gn % ns, 1 - slot):
            cp.start()

    @pl.when(s == 0)
    def _():
        qpe = qpe_ref[...]
        for h in range(heads):
            qa_scr[h * rows:(h + 1) * rows, :] = qabs_ref[h]
            qp_scr[h * rows:(h + 1) * rows, :] = qpe[:, h * ROPE_DIM:(h + 1) * ROPE_DIM]
        m_scr[...] = jnp.full(m_scr.shape, -jnp.inf, F32)
        l_scr[...] = jnp.zeros(l_scr.shape, F32)
        acc_scr[...] = jnp.zeros(acc_scr.shape, F32)

    qa = qa_scr[...].astype(BF16)
    qp = qp_scr[...].astype(BF16)

    def update(st, sc, v):
        m_old = m_scr[st]
        m_new = jnp.maximum(m_old, jnp.max(sc, axis=-1, keepdims=True))
        alpha = jnp.exp2((m_old - m_new) * c2)
        p = jnp.exp2((sc - jnp.tile(m_new, (1, sc.shape[1] // LANES))) * c2)
        psum = p[:, :LANES]
        for t in range(1, sc.shape[1] // LANES):
            psum = psum + p[:, t * LANES:(t + 1) * LANES]
        l_scr[st] = alpha * l_scr[st] + psum
        acc_scr[st] = jnp.tile(alpha, (1, nkv // LANES)) * acc_scr[st] + _dot(p.astype(BF16), v)
        m_scr[st] = m_new

    for cp in page_copies(b, s, slot):
        cp.wait()
    nst = m_scr.shape[0]
    keys = kbuf.shape[1] // nst
    for st in range(nst):
        kb = kbuf[slot, st * keys:(st + 1) * keys, :].astype(BF16)
        pt = pbuf[slot, :, st * keys:(st + 1) * keys].astype(BF16)
        update(st, _dot_nt(qa, kb) + _dot(qp, pt), kb)

    @pl.when(s == ns - 1)
    def _():
        kn = knew_ref[...]
        kn = jnp.concatenate([kn, jnp.zeros((LANES - rows, kn.shape[1]), kn.dtype)], axis=0).astype(BF16)
        kb = kn[:, :nkv]
        sc = _dot_nt(qa, kb) + _dot_nt(qp, kn[:, nkv:nkv + ROPE_DIM])
        tok = lax.broadcasted_iota(jnp.int32, sc.shape, 0) & (rows - 1)
        key = lax.broadcasted_iota(jnp.int32, sc.shape, 1)
        update(0, jnp.where((key <= tok) & (key < valid), sc, -jnp.inf), kb)
        m = m_scr[0]
        for st in range(1, nst):
            m = jnp.maximum(m, m_scr[st])
        lsum = jnp.zeros(m.shape, F32)
        acc = jnp.zeros(acc_scr.shape[1:], F32)
        for st in range(nst):
            w = jnp.exp2((m_scr[st] - m) * c2)
            lsum = lsum + w * l_scr[st]
            acc = acc + jnp.tile(w, (1, nkv // LANES)) * acc_scr[st]
        ob = (acc / jnp.sum(lsum, axis=-1, keepdims=True)).astype(BF16)
        for p in range(heads // 2):
            lo = _dot(ob, wuv_ref[2 * p])[(2 * p) * rows:(2 * p + 1) * rows]
            hi = _dot(ob, wuv_ref[2 * p + 1])[(2 * p + 1) * rows:(2 * p + 2) * rows]
            o_ref[:, p * LANES:(p + 1) * LANES] = (lo + hi).astype(o_ref.dtype)


def _attn_sample(qabs, qpe, knew, wuv, cache_ckv, cache_kpe, page_table, layer, rows, valid, scale):
    heads, T, R = qabs.shape
    B, n_pages = page_table.shape
    npg = PAGES_PER_STEP
    steps = n_pages // npg
    _, _, page, klat = cache_ckv.shape
    out_w = (heads // 2) * LANES
    kpeT = jnp.swapaxes(cache_kpe, 2, 3)

    grid_spec = pltpu.PrefetchScalarGridSpec(
        num_scalar_prefetch=1,
        grid=(B, steps),
        in_specs=[
            pl.BlockSpec((heads, rows, R), lambda b, s, pt: (0, b, 0)),
            pl.BlockSpec((rows, qpe.shape[1]), lambda b, s, pt: (b, 0)),
            pl.BlockSpec((rows, knew.shape[1]), lambda b, s, pt: (b, 0)),
            wuv.spec(),
            pl.BlockSpec(memory_space=pl.ANY),
            pl.BlockSpec(memory_space=pl.ANY),
        ],
        out_specs=pl.BlockSpec((rows, out_w), lambda b, s, pt: (b, 0)),
        scratch_shapes=[
            pltpu.VMEM((heads * rows, R), F32),
            pltpu.VMEM((heads * rows, ROPE_DIM), F32),
            pltpu.VMEM((SAMPLE_CHAINS, heads * rows, LANES), F32),
            pltpu.VMEM((SAMPLE_CHAINS, heads * rows, LANES), F32),
            pltpu.VMEM((SAMPLE_CHAINS, heads * rows, R), F32),
            pltpu.VMEM((2, npg * page, klat), F32),
            pltpu.VMEM((2, ROPE_DIM, npg * page), F32),
            pltpu.SemaphoreType.DMA((2, 2)),
        ],
    )
    return pl.pallas_call(
        functools.partial(_attn_sample_kernel, heads=heads, rows=rows, valid=valid, npages=npg, scale=scale,
                          layer=layer),
        grid_spec=grid_spec,
        out_shape=jax.ShapeDtypeStruct((T, out_w), F32),
        compiler_params=_cparams(("arbitrary", "arbitrary")),
        name="attn_sample",
    )(page_table, qabs, qpe, knew, wuv.arr, cache_ckv, kpeT)


def _ssd_kernel(xbc_ref, z_ref, dt_ref, dtT_ref, hist_ref, h0_ref, cw_ref, cb_ref,
                dtb_row_ref, dtb_col_ref, alog_row_ref, alog_col_ref, dskip_ref, gy_ref,
                y_ref, hlast_ref, xp_scr, h_scr, *, l, valid, heads, hdim, nstate, kconv):
    c = pl.program_id(1)
    nc = pl.num_programs(1)
    inner = heads * hdim
    gw = SSM_GROUPS * nstate
    hpg = heads // SSM_GROUPS
    pad = SUBLANES
    lb = xbc_ref.shape[0]

    def pad_rows(a):
        if lb == l:
            return a
        return jnp.concatenate([a, jnp.zeros((l - lb, a.shape[1]), a.dtype)], axis=0)

    @pl.when(c == 0)
    def _():
        xp_scr[0:pad, :] = hist_ref[...]
        h_scr[...] = h0_ref[...]

    xp_scr[pad:pad + l, :] = pad_rows(xbc_ref[...])
    acc = jnp.zeros((l, xbc_ref.shape[1]), F32) + cb_ref[...]
    for k in range(kconv):
        acc = acc + cw_ref[k:k + 1, :] * xp_scr[pl.ds(pad - (kconv - 1) + k, l), :]
    carry = xp_scr[l:l + pad, :]
    xp_scr[0:pad, :] = carry
    xc = _silu(acc)
    xs = xc[:, :inner]
    bm = xc[:, inner:inner + gw]
    cm = xc[:, inner + gw:inner + 2 * gw]

    dt_col = _softplus(pad_rows(dt_ref[...]) + dtb_row_ref[...])
    dt_row = _softplus(dtT_ref[...] + dtb_col_ref[...])
    if valid < l:
        dt_col = jnp.where(lax.broadcasted_iota(jnp.int32, dt_col.shape, 0) < valid, dt_col, 0.0)
        dt_row = jnp.where(lax.broadcasted_iota(jnp.int32, dt_row.shape, 1) < valid, dt_row, 0.0)
    dta_col = dt_col * (-jnp.exp(alog_row_ref[...]))
    dta_row = dt_row * (-jnp.exp(alog_col_ref[...]))

    ri = lax.broadcasted_iota(jnp.int32, (l, l), 0)
    ci = lax.broadcasted_iota(jnp.int32, (l, l), 1)
    causal = ci <= ri
    lower = jnp.where(causal, 1.0, 0.0).astype(BF16)
    upper = jnp.where(ri <= ci, 1.0, 0.0).astype(BF16)
    acs_col = sum(_dot(lower, piece) for piece in _split3(dta_col))
    acs_row = sum(_dot(piece, upper) for piece in _split3(dta_row))
    acs_last = acs_col[l - 1:l, :]
    eacs = jnp.exp(acs_col)
    toend = jnp.exp(acs_last - acs_col)

    lane = lax.broadcasted_iota(jnp.int32, (l, LANES), 1)
    per = LANES // hdim
    srow = lax.broadcasted_iota(jnp.int32, (LANES, nstate), 0)
    ys = []
    for p in range(heads // per):
        g = (p * per) // hpg
        bg = bm[:, g * nstate:(g + 1) * nstate].astype(BF16)
        cg = cm[:, g * nstate:(g + 1) * nstate]
        cbg = _dot_nt(cg.astype(BF16), bg)
        xs_p = xs[:, p * LANES:(p + 1) * LANES]
        dsk_p = dskip_ref[:, p * LANES:(p + 1) * LANES]
        hprev = h_scr[p * LANES:(p + 1) * LANES, :]
        hprev_b = hprev.astype(BF16)
        dt_p = jnp.zeros((l, LANES), F32)
        te_p = jnp.zeros((l, LANES), F32)
        cd_p = jnp.zeros((LANES, nstate), F32)
        for s in range(per):
            h = p * per + s
            sel = (lane // hdim) == s
            dt_p = jnp.where(sel, dt_col[:, h:h + 1], dt_p)
            te_p = jnp.where(sel, toend[:, h:h + 1], te_p)
            cd_p = jnp.where((srow // hdim) == s, jnp.exp(acs_row[h:h + 1, l - 1:l]), cd_p)
        dtx_p = dt_p * xs_p
        y_p = dsk_p * xs_p
        for s in range(per):
            h = p * per + s
            sel = (lane // hdim) == s
            seg = acs_col[:, h:h + 1] - acs_row[h:h + 1, :]
            dec = jnp.exp(jnp.where(causal, seg, -jnp.inf))
            mh = (cbg * dec).astype(BF16)
            y_h = _dot(mh, jnp.where(sel, dtx_p, 0.0).astype(BF16))
            cs = (cg * eacs[:, h:h + 1]).astype(BF16)
            y_h = y_h + jnp.where(sel, _dot_nt(cs, hprev_b), 0.0)
            y_p = y_p + y_h
        st_p = _dot_tn((dtx_p * te_p).astype(BF16), bg)
        h_scr[p * LANES:(p + 1) * LANES, :] = cd_p * hprev + st_p
        ys.append(y_p)
    y = jnp.concatenate(ys, axis=-1)[:lb] * _silu(z_ref[...])
    gsz = inner // SSM_GROUPS
    outs = [_rms(y[:, g * gsz:(g + 1) * gsz]) for g in range(SSM_GROUPS)]
    y_ref[...] = (jnp.concatenate(outs, axis=-1) * gy_ref[...]).astype(y_ref.dtype)

    @pl.when(c == nc - 1)
    def _():
        hlast_ref[...] = h_scr[...]


def _ssd(xbc, z, dt, dtT, hist, h0, wts, B, S, lb, valid, act_dtype):
    T, cdim = xbc.shape
    nc = S // lb
    l = SSD_CHUNK
    assert lb == l or nc == 1
    heads, hdim, nstate = wts["heads"], wts["hdim"], wts["nstate"]
    inner = heads * hdim
    kconv = wts["conv_w"].shape[0]
    row = lambda b, c: (b * nc + c, 0)
    fixed = lambda b, c: (0, 0)
    small = [wts["conv_w"], wts["conv_b"], wts["dtb_row"], wts["dtb_col"], wts["alog_row"], wts["alog_col"],
             wts["dskip"], wts["gy"]]
    return pl.pallas_call(
        functools.partial(_ssd_kernel, l=l, valid=valid, heads=heads, hdim=hdim, nstate=nstate, kconv=kconv),
        grid=(B, nc),
        in_specs=[
            pl.BlockSpec((lb, cdim), row),
            pl.BlockSpec((lb, inner), row),
            pl.BlockSpec((lb, dt.shape[1]), row),
            pl.BlockSpec((None, dtT.shape[1], l), lambda b, c: (b, 0, c)),
            pl.BlockSpec((None, None, SUBLANES, cdim), lambda b, c: (hist[1], b, 0, 0)),
            pl.BlockSpec((None, None, inner, nstate), lambda b, c: (h0[1], b, 0, 0)),
        ] + [a.spec() for a in small],
        out_specs=(pl.BlockSpec((lb, inner), row),
                   pl.BlockSpec((None, inner, nstate), lambda b, c: (b, 0, 0))),
        out_shape=(jax.ShapeDtypeStruct((T, inner), act_dtype),
                   jax.ShapeDtypeStruct((B, inner, nstate), F32)),
        scratch_shapes=[pltpu.VMEM((l + 2 * SUBLANES, cdim), F32),
                        pltpu.VMEM((inner, nstate), F32)],
        compiler_params=_cparams(("parallel", "arbitrary")),
        name="ssd",
    )(xbc, z, dt, dtT, hist[0], h0[0], *[a.arr for a in small])


def _outproj_kernel(a1_ref, a2_ref, w1_ref, w2_ref, x_ref, g_ref, b_ref, o_ref, *, alpha):
    m = _dot(a1_ref[...].astype(BF16), w1_ref[...]) + _dot(a2_ref[...].astype(BF16), w2_ref[...])
    o_ref[...] = _layer_norm(alpha * x_ref[...] + m, g_ref[...], b_ref[...])


def _outproj(a1, a2, w1, w2, x, g, b, alpha, tm):
    T, D = x.shape
    row = lambda i: (i, 0)
    return pl.pallas_call(
        functools.partial(_outproj_kernel, alpha=alpha),
        grid=(T // tm,),
        in_specs=[pl.BlockSpec((tm, a1.shape[1]), row), pl.BlockSpec((tm, a2.shape[1]), row),
                  w1.spec(), w2.spec(), pl.BlockSpec((tm, D), row), g.spec(), b.spec()],
        out_specs=pl.BlockSpec((tm, D), row),
        out_shape=jax.ShapeDtypeStruct((T, D), F32),
        compiler_params=_cparams(("parallel",)),
        name="outproj_ln",
    )(a1, a2, w1.arr, w2.arr, x, g.arr, b.arr)


def _conf_kernel(x_ref, hist_ref, wa_ref, wb_ref, ba_ref, bb_ref, dww_ref, dwb_ref, gn_ref, bn_ref, wo_ref, bo_ref,
                 g_ref, b_ref, o_ref, vlast_ref, xp_scr, *, nseq, rows, halo, ktaps, rblk, alpha):
    c = pl.program_id(1)
    nc = pl.num_programs(1)
    nlb = dww_ref.shape[0]
    first = halo - (ktaps - 1)

    @pl.when(c == 0)
    def _():
        for cb in range(nlb):
            xp_scr[cb, :, 0:halo, :] = hist_ref[:, :, cb * LANES:(cb + 1) * LANES]

    x = x_ref[...]
    xb = x.astype(BF16)
    per = 2
    convs = []
    for cc in range(nlb // per):
        sl = slice(cc * per * LANES, (cc + 1) * per * LANES)
        ua = _dot(xb, wa_ref[:, sl]) + ba_ref[:, sl]
        ub = _dot(xb, wb_ref[:, sl]) + bb_ref[:, sl]
        v = ua * jax.nn.sigmoid(ub)
        for h in range(per):
            cb = cc * per + h
            xp_scr[cb, :, halo:halo + rows, :] = v[:, h * LANES:(h + 1) * LANES].reshape(nseq, rows, LANES)
            outs = []
            for s in range(nseq):
                for rb in range(rows // rblk):
                    acc = jnp.zeros((rblk, LANES), F32) + dwb_ref[cb]
                    for k in range(ktaps):
                        acc = acc + dww_ref[cb, k:k + 1, :] * xp_scr[cb, s, pl.ds(rb * rblk + first + k, rblk), :]
                    outs.append(acc)
            convs.append(jnp.concatenate(outs, axis=0))
            tail = xp_scr[cb, :, rows:rows + halo, :]
            xp_scr[cb, :, 0:halo, :] = tail

    @pl.when(c == nc - 1)
    def _():
        for cb in range(nlb):
            vlast_ref[:, :, cb * LANES:(cb + 1) * LANES] = xp_scr[cb, :, halo:halo + rows, :]

    conv = jnp.concatenate(convs, axis=-1)
    cn = _silu(_layer_norm(conv, gn_ref[...], bn_ref[...]))
    m = _dot(cn.astype(BF16), wo_ref[...]) + bo_ref[...]
    o_ref[...] = _layer_norm(alpha * x + m, g_ref[...], b_ref[...])


def _conf(x, hist, wts, g, b, B, S, nseq, rows, alpha):
    T, D = x.shape
    nc = S // rows
    assert nseq == 1 or nc == 1
    halo = hist[0].shape[2]
    nlb, ktaps, _ = wts["dww"].shape
    C = nlb * LANES
    tmr = nseq * rows
    row = lambda bb, c: (bb * nc + c, 0)
    small = [wts[k] for k in ("wa", "wb", "ba", "bb", "dww", "dwb", "gn", "bn", "wo", "bo")] + [g, b]
    return pl.pallas_call(
        functools.partial(_conf_kernel, nseq=nseq, rows=rows, halo=halo, ktaps=ktaps, rblk=min(rows, 64),
                          alpha=alpha),
        grid=(B // nseq, nc),
        in_specs=[pl.BlockSpec((tmr, D), row),
                  pl.BlockSpec((None, nseq, halo, C), lambda bb, c: (hist[1], bb, 0, 0))]
        + [a.spec() for a in small],
        out_specs=(pl.BlockSpec((tmr, D), row), pl.BlockSpec((nseq, rows, C), lambda bb, c: (bb, 0, 0))),
        out_shape=(jax.ShapeDtypeStruct((T, D), F32), jax.ShapeDtypeStruct((B, rows, C), F32)),
        scratch_shapes=[pltpu.VMEM((nlb, nseq, rows + halo, LANES), F32)],
        compiler_params=_cparams(("parallel", "arbitrary")),
        name="conformer",
    )(x, hist[0], *[a.arr for a in small])


def _router_kernel(x_ref, wh_ref, wl_ref, rb_ref, g_ref, *, n_exp):
    x = x_ref[...]
    xh = x.astype(BF16)
    xl = (x - xh.astype(F32)).astype(BF16)
    logits = _dot_nt(wh_ref[...], xh) + _dot_nt(wl_ref[...], xh) + _dot_nt(wh_ref[...], xl)
    probs = jax.nn.sigmoid(logits)
    sel = probs + rb_ref[...]
    per = n_exp // N_GROUPS
    s = [sel[e:e + 1, :] for e in range(n_exp)]
    top = []
    for e in range(n_exp):
        g0 = (e // per) * per
        rank = jnp.zeros(s[e].shape, F32)
        for j in range(g0, g0 + per):
            if j < e:
                rank = rank + jnp.where(s[j] >= s[e], 1.0, 0.0)
            elif j > e:
                rank = rank + jnp.where(s[j] > s[e], 1.0, 0.0)
        top.append(rank < TOP_K)
    grp = []
    for g in range(N_GROUPS):
        tot = jnp.zeros(s[0].shape, F32)
        for e in range(g * per, (g + 1) * per):
            tot = tot + jnp.where(top[e], s[e], 0.0)
        grp.append(tot)
    chosen = []
    for g in range(N_GROUPS):
        ok = None
        for j in range(N_GROUPS):
            if j == g:
                continue
            t = (grp[j] < grp[g]) if j < g else (grp[j] <= grp[g])
            ok = t if ok is None else (ok & t)
        chosen.append(ok)
    w = [jnp.where(chosen[e // per] & top[e], probs[e:e + 1, :], 0.0) for e in range(n_exp)]
    den = w[0]
    for e in range(1, n_exp):
        den = den + w[e]
    for e in range(n_exp):
        g_ref[e:e + 1, :] = w[e] / den


def _router(x, wh, wl, rb, tm):
    T, D = x.shape
    E = wh.shape[0]
    return pl.pallas_call(
        functools.partial(_router_kernel, n_exp=E),
        grid=(T // tm,),
        in_specs=[pl.BlockSpec((tm, D), lambda i: (i, 0)), wh.spec(), wl.spec(), rb.spec()],
        out_specs=pl.BlockSpec((E, tm), lambda i: (0, i)),
        out_shape=jax.ShapeDtypeStruct((E, T), F32),
        compiler_params=_cparams(("parallel",)),
        name="router",
    )(x, wh.arr, wl.arr, rb.arr)


def _moe_kernel(x_ref, gates_ref, wg_ref, wu_ref, wd_ref, g_ref, b_ref, o_ref, xb_scr, acc_scr, *, alpha):
    grp = pl.program_id(1)
    ngrp = pl.num_programs(1)
    per, F, D = wd_ref.shape

    @pl.when(grp == 0)
    def _():
        xb_scr[...] = x_ref[...].astype(BF16)
        acc_scr[...] = jnp.zeros(acc_scr.shape, F32)

    xb = xb_scr[...]
    gates = gates_ref[...]
    lane = lax.broadcasted_iota(jnp.int32, gates.shape, 1)
    hs = []
    for j in range(per):
        gate = jnp.sum(jnp.where(lane == grp * per + j, gates, 0.0), axis=-1, keepdims=True)
        h = _silu(_dot(xb, wg_ref[j].astype(BF16))) * _dot(xb, wu_ref[j].astype(BF16))
        hs.append((h * gate).astype(BF16))
    acc_scr[...] += _dot(jnp.concatenate(hs, axis=-1), wd_ref[...].reshape(per * F, D).astype(BF16))

    @pl.when(grp == ngrp - 1)
    def _():
        o_ref[...] = _layer_norm(alpha * x_ref[...] + acc_scr[...], g_ref[...], b_ref[...])


def _moe(x, gates, wg, wu, wd, g, b, layer, alpha, tm):
    T, D = x.shape
    _, E, _, F = wg.shape
    per = E // N_GROUPS
    row = lambda i, e: (i, 0)
    fixed = lambda i, e: (0, 0)
    return pl.pallas_call(
        functools.partial(_moe_kernel, alpha=alpha),
        grid=(T // tm, N_GROUPS),
        in_specs=[pl.BlockSpec((tm, D), row), pl.BlockSpec((tm, E), row),
                  pl.BlockSpec((None, per, D, F), lambda i, e: (layer, e, 0, 0)),
                  pl.BlockSpec((None, per, D, F), lambda i, e: (layer, e, 0, 0)),
                  pl.BlockSpec((None, per, F, D), lambda i, e: (layer, e, 0, 0)),
                  g.spec(), b.spec()],
        out_specs=pl.BlockSpec((tm, D), row),
        out_shape=jax.ShapeDtypeStruct((T, D), F32),
        scratch_shapes=[pltpu.VMEM((tm, D), BF16), pltpu.VMEM((tm, D), F32)],
        compiler_params=_cparams(("parallel", "arbitrary")),
        name="moe",
    )(x, gates, wg, wu, wd, g.arr, b.arr)


def _rope_tables(pos, heads, reps):
    half = ROPE_DIM // 2
    inv = ROPE_BASE ** (-jnp.arange(half, dtype=F32) / half)
    ang = pos.astype(F32)[:, None] * inv[None, :]
    cos, sin = jnp.cos(ang), jnp.sin(ang)
    cos_t = jnp.tile(jnp.concatenate([cos, cos], -1), (reps, heads))
    sin_t = jnp.tile(jnp.concatenate([-sin, sin], -1), (reps, heads))
    return cos_t, sin_t


def _swap_halves(w):
    half = ROPE_DIM // 2
    return jnp.concatenate([w[..., half:], w[..., :half]], axis=-1)


def _prep_ab(w_in, g_q, g_kv, w_uq, w_uk, w_uv, conv_w, conv_b, dt_bias, a_log, d_skip, g_y, w_out):
    n_a, q_lora = g_q.shape
    kv_lora = g_kv.shape[1]
    heads = w_uq.shape[2]
    s_heads = dt_bias.shape[1]
    cdim = conv_w.shape[2]
    o1 = q_lora
    o2 = o1 + kv_lora
    o3 = o2 + ROPE_DIM
    inner = w_in.shape[2] - o3 - cdim - s_heads
    o4 = o3 + inner
    o5 = o4 + cdim
    hdim = inner // s_heads
    nstate = (cdim - inner) // (2 * SSM_GROUPS)
    wkp = w_in[:, :, o2:o3]
    dt_pad = jnp.pad(w_in[:, :, o5:], ((0, 0), (0, 0), (0, LANES - s_heads)))
    pieces = [("q", w_in[:, :, :o1]), ("kv", w_in[:, :, o1:o2]), ("kpa", jnp.tile(wkp, (1, 1, heads))),
              ("kpb", jnp.tile(_swap_halves(wkp), (1, 1, heads))), ("z", w_in[:, :, o3:o4]),
              ("xbc", w_in[:, :, o4:o5]), ("dt", dt_pad)]
    cols, off = {}, 0
    for name, p in pieces:
        cols[name] = (off, off + p.shape[2])
        off += p.shape[2]
    w_all = jnp.concatenate([p for _, p in pieces], axis=2).astype(BF16)

    wn = w_uq[..., :NOPE_DIM].reshape(n_a, q_lora, heads * NOPE_DIM).astype(BF16)
    wpe = w_uq[..., NOPE_DIM:]
    wpa = wpe.reshape(n_a, q_lora, heads * ROPE_DIM).astype(BF16)
    wpb = _swap_halves(wpe).reshape(n_a, q_lora, heads * ROPE_DIM).astype(BF16)
    per = LANES // NOPE_DIM
    slot = (jnp.arange(heads)[:, None] % per == jnp.arange(per)[None, :]).astype(F32)
    ukt = jnp.transpose(w_uk, (0, 2, 3, 1))
    wuk = (ukt[:, :, None] * slot[None, :, :, None, None]).reshape(n_a, heads, LANES, kv_lora).astype(BF16)
    v_dim = w_uv.shape[3]
    vper = LANES // v_dim
    vslot = (jnp.arange(heads)[:, None] % vper == jnp.arange(vper)[None, :]).astype(F32)
    uvt = jnp.transpose(w_uv, (0, 2, 1, 3))
    wuv = (uvt[:, :, :, None] * vslot[None, :, None, :, None]).reshape(n_a, heads, kv_lora, LANES).astype(BF16)

    def row(v, width):
        return jnp.pad(v, ((0, 0), (0, width - v.shape[1])))[:, None, :]

    def col(v, height):
        return jnp.pad(v, ((0, 0), (0, height - v.shape[1])))[:, :, None]

    ssd = dict(conv_w=conv_w, conv_b=conv_b[:, None, :],
               dtb_row=row(dt_bias, LANES), dtb_col=col(dt_bias, 2 * SUBLANES),
               alog_row=row(a_log, LANES), alog_col=col(a_log, 2 * SUBLANES),
               dskip=jnp.repeat(d_skip, hdim, axis=1)[:, None, :], gy=g_y[:, None, :])
    return dict(cols=cols, w_all=w_all, gq=g_q[:, None, :], gkv=g_kv[:, None, :], wn=wn, wpa=wpa, wpb=wpb,
                wuk=wuk, wuv=wuv, ssd=ssd, w_out=w_out.astype(BF16), n_mla=heads * v_dim,
                dims=dict(heads=s_heads, hdim=hdim, nstate=nstate))


def _ab_layer(st, i):
    lay = lambda name: _layer(st[name], i)
    n_out = st["w_out"].shape[1]
    assert 2 * st["n_mla"] == n_out
    ssd = dict(st["dims"], **{k: _layer(v, i) for k, v in st["ssd"].items()})
    return dict(cols=st["cols"], w_all=lay("w_all"), gq=lay("gq"), gkv=lay("gkv"), wn=lay("wn"), wpa=lay("wpa"),
                wpb=lay("wpb"), wuk=lay("wuk"), wuv=lay("wuv"), ssd=ssd,
                w_out_a=_layer(st["w_out"], i, axis=1, part=0, nparts=2),
                w_out_b=_layer(st["w_out"], i, axis=1, part=1, nparts=2))


def _mixer_ab(x, grp, wts, tabs, ln_g, ln_b, alpha, paged=None):
    B, S, tm, l, valid, act = grp["B"], grp["S"], grp["tm"], grp["l"], grp["valid"], grp["act"]
    cos_t, sin_t = tabs
    hq, ckv, kpe, kcat, z, xbc, dt = _inproj(x, wts["w_all"], wts["gq"], wts["gkv"], cos_t, sin_t,
                                            wts["cols"], tm, act)
    qabs, qpe = _qprep(hq, wts["wn"], wts["wpa"], wts["wpb"], wts["wuk"], cos_t, sin_t, tm, act)
    scale = (NOPE_DIM + ROPE_DIM) ** -0.5
    if paged is None:
        o_mla = _attn_prompt(qabs, qpe, kcat, wts["wuv"], B, S, grp["tq"], grp["tk"], scale, act)
    else:
        cache_ckv, cache_kpe, page_table, idx = paged
        o_mla = _attn_sample(qabs, qpe, kcat, wts["wuv"], cache_ckv, cache_kpe, page_table, idx, S, valid, scale)
    s_heads = wts["ssd"]["heads"]
    dtT = jnp.swapaxes(dt[:, :s_heads].reshape(B, S, s_heads), 1, 2)
    dtT = jnp.pad(dtT, ((0, 0), (0, 2 * SUBLANES - s_heads), (0, max(SSD_CHUNK - S, 0))))
    y, h_last = _ssd(xbc, z, dt, dtT, grp["ssm_hist"], grp["ssm_h0"], wts["ssd"], B, S, l, valid, act)
    x_new = _outproj(o_mla, y, wts["w_out_a"], wts["w_out_b"], x, ln_g, ln_b, alpha, tm)
    return x_new, ckv, kpe, h_last, xbc


def _mixer_c(x, grp, wts, ln_g, ln_b, alpha):
    return _conf(x, grp["conf_hist"], wts, ln_g, ln_b, grp["B"], grp["S"], grp["conf_nseq"], grp["conf_rows"], alpha)


def _moe_block(x, rt, wg, wu, wd, ln_g, ln_b, layer, alpha, tm_r, tm_m):
    gates_t = _router(x, rt["wh"], rt["wl"], rt["rb"], tm_r)
    return _moe(x, gates_t.T, wg, wu, wd, ln_g, ln_b, layer, alpha, tm_m)


def kernel(x_prompt, x_sample, cache_ckv, cache_kpe, state_ssm, state_ssm_conv, state_conf_conv, page_table,
           w_in_ab, g_q_norm, g_kv_norm, w_uq, w_uk, w_uv, ssm_conv_w, ssm_conv_b, ssm_dt_bias, ssm_a_log,
           ssm_d, ssm_norm_g, w_out_ab, conf_w_in, conf_b_in, conf_dw_w, conf_dw_b, conf_norm_g, conf_norm_b,
           conf_w_out, conf_b_out, ln_mix_g, ln_mix_b, ln_ffn_g, ln_ffn_b, w_router, router_bias,
           moe_w_gate, moe_w_up, moe_w_down):
    bp, S, D = x_prompt.shape
    bd, t_new, _ = x_sample.shape
    depth = ln_mix_g.shape[0]
    past_len = page_table.shape[1] * PAGE_SIZE
    alpha = (2 * depth) ** 0.25
    heads = w_uq.shape[2]
    n_a = w_in_ab.shape[0]
    n_c = conf_w_in.shape[0]
    conf_ch = conf_dw_w.shape[2]
    conf_k = conf_dw_w.shape[1]
    kconv = ssm_conv_w.shape[1]
    cdim = ssm_conv_w.shape[2]
    rows = SAMPLE_ROWS
    tm_p = 512

    ab_st = _prep_ab(w_in_ab, g_q_norm, g_kv_norm, w_uq, w_uk, w_uv, ssm_conv_w, ssm_conv_b, ssm_dt_bias,
                     ssm_a_log, ssm_d, ssm_norm_g, w_out_ab)
    ab = [_ab_layer(ab_st, i) for i in range(n_a)]
    nlb = conf_ch // LANES
    cw_in = conf_w_in.astype(BF16)
    cb_in = conf_b_in[:, None, :]
    cw_dw = jnp.transpose(conf_dw_w.reshape(n_c, conf_k, nlb, LANES), (0, 2, 1, 3))
    cb_dw = conf_dw_b.reshape(n_c, nlb, 1, LANES)
    cw_out = conf_w_out.astype(BF16)
    c_gn, c_bn, cb_out = conf_norm_g[:, None, :], conf_norm_b[:, None, :], conf_b_out[:, None, :]
    cw = [dict(wa=_layer(cw_in, i, axis=2, part=0, nparts=2), wb=_layer(cw_in, i, axis=2, part=1, nparts=2),
               ba=_layer(cb_in, i, axis=2, part=0, nparts=2), bb=_layer(cb_in, i, axis=2, part=1, nparts=2),
               dww=_layer(cw_dw, i), dwb=_layer(cb_dw, i), gn=_layer(c_gn, i), bn=_layer(c_bn, i),
               wo=_layer(cw_out, i), bo=_layer(cb_out, i))
          for i in range(n_c)]
    wr_t = w_router.T
    wr_h = wr_t.astype(BF16)
    rt = dict(wh=_W(wr_h), wl=_W((wr_t - wr_h.astype(F32)).astype(BF16)), rb=_W(router_bias[:, None]))
    wg_b, wu_b, wd_b = moe_w_gate, moe_w_up, moe_w_down
    ln_mg, ln_mb = ln_mix_g[:, None, :], ln_mix_b[:, None, :]
    ln_fg, ln_fb = ln_ffn_g[:, None, :], ln_ffn_b[:, None, :]

    halo = 4 * SUBLANES
    sub_heads = ssm_dt_bias.shape[1]
    inner = state_ssm.shape[2] * state_ssm.shape[3]
    nstate = state_ssm.shape[4]
    gp = dict(B=bp, S=S, tm=tm_p, tq=256, tk=512, l=min(SSD_CHUNK, S), valid=min(SSD_CHUNK, S), act=BF16,
              ssm_hist=(jnp.zeros((1, bp, SUBLANES, cdim), F32), 0),
              ssm_h0=(jnp.zeros((1, bp, inner, nstate), F32), 0),
              conf_hist=(jnp.zeros((1, bp, halo, conf_ch), F32), 0), conf_nseq=1, conf_rows=min(256, S))
    ssm_hist_s = jnp.pad(state_ssm_conv, ((0, 0), (0, 0), (SUBLANES - (kconv - 1), 0), (0, 0)))
    ssm_h0_s = state_ssm.reshape(n_a, bd, inner, nstate)
    conf_hist_s = jnp.pad(state_conf_conv, ((0, 0), (0, 0), (halo - (conf_k - 1), 0), (0, 0)))
    gs = dict(B=bd, S=rows, tm=bd * rows, l=rows, valid=t_new, act=F32, conf_nseq=bd, conf_rows=rows)
    tabs_p = _rope_tables(jnp.arange(S, dtype=jnp.int32), heads, 1)
    pos_s = past_len + jnp.arange(rows, dtype=jnp.int32)
    tabs_s = _rope_tables(pos_s, heads, bd)

    hp = x_prompt.reshape(bp * S, D)
    hs = jnp.pad(x_sample, ((0, 0), (0, rows - t_new), (0, 0))).reshape(bd * rows, D)
    out_p = dict(ckv=[], kpe=[], ssm=[], sconv=[], cconv=[])
    out_s = dict(ckv=[], kpe=[], ssm=[], sconv=[], cconv=[])
    for layer in range(depth):
        i = layer // 2
        lg, lb = _layer(ln_mg, layer), _layer(ln_mb, layer)
        if layer % 2 == 0:
            hp, c1, k1, h1, xbc1 = _mixer_ab(hp, gp, ab[i], tabs_p, lg, lb, alpha)
            gs_l = dict(gs, ssm_hist=(ssm_hist_s, i), ssm_h0=(ssm_h0_s, i))
            hs, c2, k2, h2, xbc2 = _mixer_ab(hs, gs_l, ab[i], tabs_s, lg, lb, alpha,
                                             paged=(cache_ckv, cache_kpe, page_table, i))
            out_p["ckv"].append(c1.reshape(bp, S, -1))
            out_p["kpe"].append(k1.reshape(bp, S, -1)[..., :ROPE_DIM])
            out_p["ssm"].append(h1.reshape(bp, sub_heads, -1, nstate))
            out_p["sconv"].append(xbc1.reshape(bp, S, cdim)[:, S - (kconv - 1):])
            out_s["ckv"].append(c2.reshape(bd, rows, -1)[:, :t_new])
            out_s["kpe"].append(k2.reshape(bd, rows, -1)[:, :t_new, :ROPE_DIM])
            out_s["ssm"].append(h2.reshape(bd, sub_heads, -1, nstate))
            xp = jnp.concatenate([state_ssm_conv[i], xbc2.reshape(bd, rows, cdim)[:, :t_new]], axis=1)
            out_s["sconv"].append(xp[:, xp.shape[1] - (kconv - 1):])
        else:
            hp, v1 = _mixer_c(hp, gp, cw[i], lg, lb, alpha)
            gs_l = dict(gs, conf_hist=(conf_hist_s, i))
            hs, v2 = _mixer_c(hs, gs_l, cw[i], lg, lb, alpha)
            out_p["cconv"].append(v1[:, v1.shape[1] - (conf_k - 1):])
            vp = jnp.concatenate([state_conf_conv[i], v2[:, :t_new]], axis=1)
            out_s["cconv"].append(vp[:, vp.shape[1] - (conf_k - 1):])
        fg, fb = _layer(ln_fg, layer), _layer(ln_fb, layer)
        hp = _moe_block(hp, rt, wg_b, wu_b, wd_b, fg, fb, layer, alpha, 512, 1024)
        hs = _moe_block(hs, rt, wg_b, wu_b, wd_b, fg, fb, layer, alpha, bd * rows, bd * rows)
    y_p = hp.reshape(bp, S, D)
    y_s = hs.reshape(bd, rows, D)[:, :t_new]
    return (y_p, y_s,
            jnp.stack(out_p["ckv"]), jnp.stack(out_p["kpe"]), jnp.stack(out_p["ssm"]),
            jnp.stack(out_p["sconv"]), jnp.stack(out_p["cconv"]),
            jnp.stack(out_s["ckv"]), jnp.stack(out_s["kpe"]), jnp.stack(out_s["ssm"]),
            jnp.stack(out_s["sconv"]), jnp.stack(out_s["cconv"]))
```

```python
import functools
import math

import jax
import jax.numpy as jnp
from jax import lax
from jax.experimental import pallas as pl
from jax.experimental.pallas import tpu as pltpu

F32 = jnp.float32
BF16 = jnp.bfloat16

PAGE_SIZE = 128
NOPE_DIM = 64
ROPE_DIM = 32
ROPE_BASE = 10000.0
SSM_GROUPS = 2
SSD_CHUNK = 128
N_GROUPS = 4
TOP_K = 2
EPS = 1e-6

LANES = 128
SUBLANES = 8
VMEM_LIMIT_BYTES = 56 * 1024 * 1024
SAMPLE_ROWS = SUBLANES
PAGES_PER_STEP = 16
SAMPLE_CHAINS = 2


def _dot(a, b):
    return jnp.dot(a, b, preferred_element_type=F32)


def _dot_nt(a, b):
    return lax.dot_general(a, b, (((1,), (1,)), ((), ())), preferred_element_type=F32)


def _dot_tn(a, b):
    return lax.dot_general(a, b, (((0,), (0,)), ((), ())), preferred_element_type=F32)


def _rms(x):
    return x * lax.rsqrt(jnp.mean(x * x, axis=-1, keepdims=True) + EPS)


def _layer_norm(r, g, b):
    mu = jnp.mean(r, axis=-1, keepdims=True)
    d = r - mu
    var = jnp.mean(d * d, axis=-1, keepdims=True)
    return d * lax.rsqrt(var + EPS) * g + b


def _silu(x):
    return x * jax.nn.sigmoid(x)


def _softplus(x):
    return jnp.maximum(x, 0.0) + jnp.log1p(jnp.exp(-jnp.abs(x)))


def _split3(x):
    hi = x.astype(BF16)
    r1 = x - hi.astype(F32)
    mid = r1.astype(BF16)
    lo = (r1 - mid.astype(F32)).astype(BF16)
    return hi, mid, lo


def _cparams(sem):
    return pltpu.CompilerParams(dimension_semantics=sem, vmem_limit_bytes=VMEM_LIMIT_BYTES)


class _W:
    def __init__(self, arr, block=None, index=None):
        self.arr = arr
        self.block = tuple(arr.shape) if block is None else tuple(block)
        self.index = (0,) * arr.ndim if index is None else tuple(index)
        self.shape = tuple(b for b in self.block if b is not None)

    def spec(self):
        index = self.index
        return pl.BlockSpec(self.block, lambda *_: index)


def _layer(arr, i, axis=None, part=0, nparts=1):
    block = [None] + list(arr.shape[1:])
    index = [i] + [0] * (arr.ndim - 1)
    if axis is not None:
        block[axis] = arr.shape[axis] // nparts
        index[axis] = part
    return _W(arr, block, index)


def _inproj_kernel(x_ref, w_ref, gq_ref, gkv_ref, cos_ref, sin_ref,
                   hq_ref, ckv_ref, kpe_ref, kcat_ref, z_ref, xbc_ref, dt_ref, *, cols):
    xb = x_ref[...].astype(BF16)

    def proj(name):
        a, b = cols[name]
        return _dot(xb, w_ref[:, a:b])

    hq_ref[...] = (_rms(proj("q")) * gq_ref[...]).astype(hq_ref.dtype)
    ckv = _rms(proj("kv")) * gkv_ref[...]
    ckv_ref[...] = ckv
    kpe = proj("kpa") * cos_ref[...] + proj("kpb") * sin_ref[...]
    kpe_ref[...] = kpe
    nkv = ckv.shape[-1]
    kcat_ref[:, :nkv] = ckv.astype(kcat_ref.dtype)
    kcat_ref[:, nkv:] = kpe.astype(kcat_ref.dtype)
    z_ref[...] = proj("z")
    xbc_ref[...] = proj("xbc")
    dt_ref[...] = proj("dt")


def _inproj(x, w, gq, gkv, cos_t, sin_t, cols, tm, act_dtype):
    T, D = x.shape
    n_tab = cos_t.shape[0] // tm
    widths = {k: b - a for k, (a, b) in cols.items()}
    row = lambda i: (i, 0)
    tab = lambda i: (i % n_tab, 0)
    out_shapes = (
        jax.ShapeDtypeStruct((T, widths["q"]), act_dtype),
        jax.ShapeDtypeStruct((T, widths["kv"]), F32),
        jax.ShapeDtypeStruct((T, widths["kpa"]), F32),
        jax.ShapeDtypeStruct((T, widths["kv"] + widths["kpa"]), act_dtype),
        jax.ShapeDtypeStruct((T, widths["z"]), F32),
        jax.ShapeDtypeStruct((T, widths["xbc"]), F32),
        jax.ShapeDtypeStruct((T, widths["dt"]), F32),
    )
    return pl.pallas_call(
        functools.partial(_inproj_kernel, cols=cols),
        grid=(T // tm,),
        in_specs=[
            pl.BlockSpec((tm, D), row),
            w.spec(), gq.spec(), gkv.spec(),
            pl.BlockSpec((tm, cos_t.shape[1]), tab),
            pl.BlockSpec((tm, sin_t.shape[1]), tab),
        ],
        out_specs=tuple(pl.BlockSpec((tm, s.shape[1]), row) for s in out_shapes),
        out_shape=out_shapes,
        compiler_params=_cparams(("parallel",)),
        name="inproj",
    )(x, w.arr, gq.arr, gkv.arr, cos_t, sin_t)


def _qprep_kernel(hq_ref, wn_ref, wpa_ref, wpb_ref, wuk_ref, cos_ref, sin_ref, qabs_ref, qpe_ref, *, heads):
    hq = hq_ref[...].astype(BF16)
    qn = _dot(hq, wn_ref[...])
    qpe = _dot(hq, wpa_ref[...]) * cos_ref[...] + _dot(hq, wpb_ref[...]) * sin_ref[...]
    qpe_ref[...] = qpe.astype(qpe_ref.dtype)
    per = LANES // NOPE_DIM
    for h in range(heads):
        slab = qn[:, (h // per) * LANES:(h // per + 1) * LANES].astype(BF16)
        qabs_ref[h] = _dot(slab, wuk_ref[h]).astype(qabs_ref.dtype)


def _qprep(hq, wn, wpa, wpb, wuk, cos_t, sin_t, tm, act_dtype):
    T, Q = hq.shape
    heads, _, R = wuk.shape
    n_tab = cos_t.shape[0] // tm
    row = lambda i: (i, 0)
    tab = lambda i: (i % n_tab, 0)
    return pl.pallas_call(
        functools.partial(_qprep_kernel, heads=heads),
        grid=(T // tm,),
        in_specs=[
            pl.BlockSpec((tm, Q), row),
            wn.spec(), wpa.spec(), wpb.spec(), wuk.spec(),
            pl.BlockSpec((tm, cos_t.shape[1]), tab),
            pl.BlockSpec((tm, sin_t.shape[1]), tab),
        ],
        out_specs=(pl.BlockSpec((heads, tm, R), lambda i: (0, i, 0)),
                   pl.BlockSpec((tm, wpa.shape[1]), row)),
        out_shape=(jax.ShapeDtypeStruct((heads, T, R), act_dtype),
                   jax.ShapeDtypeStruct((T, wpa.shape[1]), act_dtype)),
        compiler_params=_cparams(("parallel",)),
        name="qprep",
    )(hq, wn.arr, wpa.arr, wpb.arr, wuk.arr, cos_t, sin_t)


def _attn_kernel(qi_ref, kj_ref, qabs_ref, qpe_ref, k_ref, wuv_ref, o_ref,
                 q_scr, m_scr, l_scr, acc_scr, *, heads, tq, tk, scale):
    s = pl.program_id(1)
    i = qi_ref[s]
    j = kj_ref[s]
    nkv = qabs_ref.shape[-1]
    c2 = scale * math.log2(math.e)

    @pl.when(j == 0)
    def _():
        qpe = qpe_ref[...]
        lane = lax.broadcasted_iota(jnp.int32, qpe.shape, 1)
        for h in range(heads):
            q_scr[h * tq:(h + 1) * tq, :nkv] = qabs_ref[h].astype(BF16)
            q_scr[h * tq:(h + 1) * tq, nkv:] = jnp.where(lane // ROPE_DIM == h, qpe, 0).astype(BF16)
        m_scr[...] = jnp.full(m_scr.shape, -jnp.inf, F32)
        l_scr[...] = jnp.zeros(l_scr.shape, F32)
        acc_scr[...] = jnp.zeros(acc_scr.shape, F32)

    k = k_ref[...]

    def update(sc):
        m_old = m_scr[...]
        m_new = jnp.maximum(m_old, jnp.max(sc, axis=-1, keepdims=True))
        alpha = jnp.exp2((m_old - m_new) * c2)
        p = jnp.exp2((sc - jnp.tile(m_new, (1, tk // LANES))) * c2)
        psum = p[:, :LANES]
        for t in range(1, tk // LANES):
            psum = psum + p[:, t * LANES:(t + 1) * LANES]
        l_scr[...] = alpha * l_scr[...] + psum
        acc_scr[...] = jnp.tile(alpha, (1, nkv // LANES)) * acc_scr[...] + _dot(p.astype(BF16), k[:, :nkv])
        m_scr[...] = m_new

    diag = ((i + 1) * tq - 1) // tk

    @pl.when(j < diag)
    def _():
        update(_dot_nt(q_scr[...], k))

    @pl.when(j == diag)
    def _():
        sc = _dot_nt(q_scr[...], k)
        r = lax.broadcasted_iota(jnp.int32, sc.shape, 0)
        c = lax.broadcasted_iota(jnp.int32, sc.shape, 1)
        update(jnp.where(j * tk + c <= i * tq + (r & (tq - 1)), sc, -jnp.inf))
        o = acc_scr[...] / jnp.sum(l_scr[...], axis=-1, keepdims=True)
        for p in range(heads // 2):
            lo = o[(2 * p) * tq:(2 * p + 1) * tq].astype(BF16)
            hi = o[(2 * p + 1) * tq:(2 * p + 2) * tq].astype(BF16)
            o_ref[:, p * LANES:(p + 1) * LANES] = (
                _dot(lo, wuv_ref[2 * p]) + _dot(hi, wuv_ref[2 * p + 1])).astype(o_ref.dtype)


def _attn_prompt(qabs, qpe, kcat, wuv, B, S, tq, tk, scale, act_dtype):
    heads, T, R = qabs.shape
    assert tk % tq == 0 and S % tk == 0
    nq = S // tq
    nk = S // tk
    pairs = [(i, j) for i in range(nq) for j in range(((i + 1) * tq - 1) // tk + 1)]
    qi = jnp.asarray([p[0] for p in pairs], jnp.int32)
    kj = jnp.asarray([p[1] for p in pairs], jnp.int32)
    kw = kcat.shape[1]
    out_w = (heads // 2) * LANES
    grid_spec = pltpu.PrefetchScalarGridSpec(
        num_scalar_prefetch=2,
        grid=(B, len(pairs)),
        in_specs=[
            pl.BlockSpec((heads, tq, R), lambda b, s, qi, kj: (0, b * nq + qi[s], 0)),
            pl.BlockSpec((tq, qpe.shape[1]), lambda b, s, qi, kj: (b * nq + qi[s], 0)),
            pl.BlockSpec((tk, kw), lambda b, s, qi, kj: (b * nk + kj[s], 0)),
            wuv.spec(),
        ],
        out_specs=pl.BlockSpec((tq, out_w), lambda b, s, qi, kj: (b * nq + qi[s], 0)),
        scratch_shapes=[
            pltpu.VMEM((heads * tq, kw), BF16),
            pltpu.VMEM((heads * tq, LANES), F32),
            pltpu.VMEM((heads * tq, LANES), F32),
            pltpu.VMEM((heads * tq, R), F32),
        ],
    )
    return pl.pallas_call(
        functools.partial(_attn_kernel, heads=heads, tq=tq, tk=tk, scale=scale),
        grid_spec=grid_spec,
        out_shape=jax.ShapeDtypeStruct((T, out_w), act_dtype),
        compiler_params=_cparams(("parallel", "arbitrary")),
        name="attn_prompt",
    )(qi, kj, qabs, qpe, kcat, wuv.arr)


def _attn_sample_kernel(pt_ref, qabs_ref, qpe_ref, knew_ref, wuv_ref, ckv_hbm, kpe_hbm, o_ref,
                        qa_scr, qp_scr, m_scr, l_scr, acc_scr, kbuf, pbuf, sem,
                        *, heads, rows, valid, npages, scale, layer):
    b = pl.program_id(0)
    s = pl.program_id(1)
    nb = pl.num_programs(0)
    ns = pl.num_programs(1)
    nkv = qabs_ref.shape[-1]
    page = kbuf.shape[1] // npages
    c2 = scale * math.log2(math.e)
    g = b * ns + s
    slot = g % 2

    def page_copies(bb, ss, sl):
        out = []
        for p in range(npages):
            pg = pt_ref[bb, ss * npages + p]
            out.append(pltpu.make_async_copy(ckv_hbm.at[layer, pg], kbuf.at[sl, pl.ds(p * page, page), :],
                                             sem.at[0, sl]))
            out.append(pltpu.make_async_copy(kpe_hbm.at[layer, pg], pbuf.at[sl, :, pl.ds(p * page, page)],
                                             sem.at[1, sl]))
        return out

    @pl.when(g == 0)
    def _():
        for cp in page_copies(b, s, slot):
            cp.start()

    @pl.when(g + 1 < nb * ns)
    def _():
        gn = g + 1
        for cp in page_copies(gn // ns, gn % ns, 1 - slot):
            cp.start()

    @pl.when(s == 0)
    def _():
        qpe = qpe_ref[...]
        for h in range(heads):
            qa_scr[h * rows:(h + 1) * rows, :] = qabs_ref[h]
            qp_scr[h * rows:(h + 1) * rows, :] = qpe[:, h * ROPE_DIM:(h + 1) * ROPE_DIM]
        m_scr[...] = jnp.full(m_scr.shape, -jnp.inf, F32)
        l_scr[...] = jnp.zeros(l_scr.shape, F32)
        acc_scr[...] = jnp.zeros(acc_scr.shape, F32)

    qa = qa_scr[...].astype(BF16)
    qp = qp_scr[...].astype(BF16)

    def update(st, sc, v):
        m_old = m_scr[st]
        m_new = jnp.maximum(m_old, jnp.max(sc, axis=-1, keepdims=True))
        alpha = jnp.exp2((m_old - m_new) * c2)
        p = jnp.exp2((sc - jnp.tile(m_new, (1, sc.shape[1] // LANES))) * c2)
        psum = p[:, :LANES]
        for t in range(1, sc.shape[1] // LANES):
            psum = psum + p[:, t * LANES:(t + 1) * LANES]
        l_scr[st] = alpha * l_scr[st] + psum
        acc_scr[st] = jnp.tile(alpha, (1, nkv // LANES)) * acc_scr[st] + _dot(p.astype(BF16), v)
        m_scr[st] = m_new

    for cp in page_copies(b, s, slot):
        cp.wait()
    nst = m_scr.shape[0]
    keys = kbuf.shape[1] // nst
    for st in range(nst):
        kb = kbuf[slot, st * keys:(st + 1) * keys, :].astype(BF16)
        pt = pbuf[slot, :, st * keys:(st + 1) * keys].astype(BF16)
        update(st, _dot_nt(qa, kb) + _dot(qp, pt), kb)

    @pl.when(s == ns - 1)
    def _():
        kn = knew_ref[...]
        kn = jnp.concatenate([kn, jnp.zeros((LANES - rows, kn.shape[1]), kn.dtype)], axis=0).astype(BF16)
        kb = kn[:, :nkv]
        sc = _dot_nt(qa, kb) + _dot_nt(qp, kn[:, nkv:nkv + ROPE_DIM])
        tok = lax.broadcasted_iota(jnp.int32, sc.shape, 0) & (rows - 1)
        key = lax.broadcasted_iota(jnp.int32, sc.shape, 1)
        update(0, jnp.where((key <= tok) & (key < valid), sc, -jnp.inf), kb)
        m = m_scr[0]
        for st in range(1, nst):
            m = jnp.maximum(m, m_scr[st])
        lsum = jnp.zeros(m.shape, F32)
        acc = jnp.zeros(acc_scr.shape[1:], F32)
        for st in range(nst):
            w = jnp.exp2((m_scr[st] - m) * c2)
            lsum = lsum + w * l_scr[st]
            acc = acc + jnp.tile(w, (1, nkv // LANES)) * acc_scr[st]
        ob = (acc / jnp.sum(lsum, axis=-1, keepdims=True)).astype(BF16)
        for p in range(heads // 2):
            lo = _dot(ob, wuv_ref[2 * p])[(2 * p) * rows:(2 * p + 1) * rows]
            hi = _dot(ob, wuv_ref[2 * p + 1])[(2 * p + 1) * rows:(2 * p + 2) * rows]
            o_ref[:, p * LANES:(p + 1) * LANES] = (lo + hi).astype(o_ref.dtype)


def _attn_sample(qabs, qpe, knew, wuv, cache_ckv, cache_kpe, page_table, layer, rows, valid, scale):
    heads, T, R = qabs.shape
    B, n_pages = page_table.shape
    npg = PAGES_PER_STEP
    steps = n_pages // npg
    _, _, page, klat = cache_ckv.shape
    out_w = (heads // 2) * LANES
    kpeT = jnp.swapaxes(cache_kpe, 2, 3)

    grid_spec = pltpu.PrefetchScalarGridSpec(
        num_scalar_prefetch=1,
        grid=(B, steps),
        in_specs=[
            pl.BlockSpec((heads, rows, R), lambda b, s, pt: (0, b, 0)),
            pl.BlockSpec((rows, qpe.shape[1]), lambda b, s, pt: (b, 0)),
            pl.BlockSpec((rows, knew.shape[1]), lambda b, s, pt: (b, 0)),
            wuv.spec(),
            pl.BlockSpec(memory_space=pl.ANY),
            pl.BlockSpec(memory_space=pl.ANY),
        ],
        out_specs=pl.BlockSpec((rows, out_w), lambda b, s, pt: (b, 0)),
        scratch_shapes=[
            pltpu.VMEM((heads * rows, R), F32),
            pltpu.VMEM((heads * rows, ROPE_DIM), F32),
            pltpu.VMEM((SAMPLE_CHAINS, heads * rows, LANES), F32),
            pltpu.VMEM((SAMPLE_CHAINS, heads * rows, LANES), F32),
            pltpu.VMEM((SAMPLE_CHAINS, heads * rows, R), F32),
            pltpu.VMEM((2, npg * page, klat), F32),
            pltpu.VMEM((2, ROPE_DIM, npg * page), F32),
            pltpu.SemaphoreType.DMA((2, 2)),
        ],
    )
    return pl.pallas_call(
        functools.partial(_attn_sample_kernel, heads=heads, rows=rows, valid=valid, npages=npg, scale=scale,
                          layer=layer),
        grid_spec=grid_spec,
        out_shape=jax.ShapeDtypeStruct((T, out_w), F32),
        compiler_params=_cparams(("arbitrary", "arbitrary")),
        name="attn_sample",
    )(page_table, qabs, qpe, knew, wuv.arr, cache_ckv, kpeT)


def _ssd_kernel(xbc_ref, z_ref, dt_ref, dtT_ref, hist_ref, h0_ref, cw_ref, cb_ref,
                dtb_row_ref, dtb_col_ref, alog_row_ref, alog_col_ref, dskip_ref, gy_ref,
                y_ref, hlast_ref, xp_scr, h_scr, *, l, valid, heads, hdim, nstate, kconv):
    c = pl.program_id(1)
    nc = pl.num_programs(1)
    inner = heads * hdim
    gw = SSM_GROUPS * nstate
    hpg = heads // SSM_GROUPS
    pad = SUBLANES
    lb = xbc_ref.shape[0]

    def pad_rows(a):
        if lb == l:
            return a
        return jnp.concatenate([a, jnp.zeros((l - lb, a.shape[1]), a.dtype)], axis=0)

    @pl.when(c == 0)
    def _():
        xp_scr[0:pad, :] = hist_ref[...]
        h_scr[...] = h0_ref[...]

    xp_scr[pad:pad + l, :] = pad_rows(xbc_ref[...])
    acc = jnp.zeros((l, xbc_ref.shape[1]), F32) + cb_ref[...]
    for k in range(kconv):
        acc = acc + cw_ref[k:k + 1, :] * xp_scr[pl.ds(pad - (kconv - 1) + k, l), :]
    carry = xp_scr[l:l + pad, :]
    xp_scr[0:pad, :] = carry
    xc = _silu(acc)
    xs = xc[:, :inner]
    bm = xc[:, inner:inner + gw]
    cm = xc[:, inner + gw:inner + 2 * gw]

    dt_col = _softplus(pad_rows(dt_ref[...]) + dtb_row_ref[...])
    dt_row = _softplus(dtT_ref[...] + dtb_col_ref[...])
    if valid < l:
        dt_col = jnp.where(lax.broadcasted_iota(jnp.int32, dt_col.shape, 0) < valid, dt_col, 0.0)
        dt_row = jnp.where(lax.broadcasted_iota(jnp.int32, dt_row.shape, 1) < valid, dt_row, 0.0)
    dta_col = dt_col * (-jnp.exp(alog_row_ref[...]))
    dta_row = dt_row * (-jnp.exp(alog_col_ref[...]))

    ri = lax.broadcasted_iota(jnp.int32, (l, l), 0)
    ci = lax.broadcasted_iota(jnp.int32, (l, l), 1)
    causal = ci <= ri
    lower = jnp.where(causal, 1.0, 0.0).astype(BF16)
    upper = jnp.where(ri <= ci, 1.0, 0.0).astype(BF16)
    acs_col = sum(_dot(lower, piece) for piece in _split3(dta_col))
    acs_row = sum(_dot(piece, upper) for piece in _split3(dta_row))
    acs_last = acs_col[l - 1:l, :]
    eacs = jnp.exp(acs_col)
    toend = jnp.exp(acs_last - acs_col)

    lane = lax.broadcasted_iota(jnp.int32, (l, LANES), 1)
    per = LANES // hdim
    srow = lax.broadcasted_iota(jnp.int32, (LANES, nstate), 0)
    ys = []
    for p in range(heads // per):
        g = (p * per) // hpg
        bg = bm[:, g * nstate:(g + 1) * nstate].astype(BF16)
        cg = cm[:, g * nstate:(g + 1) * nstate]
        cbg = _dot_nt(cg.astype(BF16), bg)
        xs_p = xs[:, p * LANES:(p + 1) * LANES]
        dsk_p = dskip_ref[:, p * LANES:(p + 1) * LANES]
        hprev = h_scr[p * LANES:(p + 1) * LANES, :]
        hprev_b = hprev.astype(BF16)
        dt_p = jnp.zeros((l, LANES), F32)
        te_p = jnp.zeros((l, LANES), F32)
        cd_p = jnp.zeros((LANES, nstate), F32)
        for s in range(per):
            h = p * per + s
            sel = (lane // hdim) == s
            dt_p = jnp.where(sel, dt_col[:, h:h + 1], dt_p)
            te_p = jnp.where(sel, toend[:, h:h + 1], te_p)
            cd_p = jnp.where((srow // hdim) == s, jnp.exp(acs_row[h:h + 1, l - 1:l]), cd_p)
        dtx_p = dt_p * xs_p
        y_p = dsk_p * xs_p
        for s in range(per):
            h = p * per + s
            sel = (lane // hdim) == s
            seg = acs_col[:, h:h + 1] - acs_row[h:h + 1, :]
            dec = jnp.exp(jnp.where(causal, seg, -jnp.inf))
            mh = (cbg * dec).astype(BF16)
            y_h = _dot(mh, jnp.where(sel, dtx_p, 0.0).astype(BF16))
            cs = (cg * eacs[:, h:h + 1]).astype(BF16)
            y_h = y_h + jnp.where(sel, _dot_nt(cs, hprev_b), 0.0)
            y_p = y_p + y_h
        st_p = _dot_tn((dtx_p * te_p).astype(BF16), bg)
        h_scr[p * LANES:(p + 1) * LANES, :] = cd_p * hprev + st_p
        ys.append(y_p)
    y = jnp.concatenate(ys, axis=-1)[:lb] * _silu(z_ref[...])
    gsz = inner // SSM_GROUPS
    outs = [_rms(y[:, g * gsz:(g + 1) * gsz]) for g in range(SSM_GROUPS)]
    y_ref[...] = (jnp.concatenate(outs, axis=-1) * gy_ref[...]).astype(y_ref.dtype)

    @pl.when(c == nc - 1)
    def _():
        hlast_ref[...] = h_scr[...]


def _ssd(xbc, z, dt, dtT, hist, h0, wts, B, S, lb, valid, act_dtype):
    T, cdim = xbc.shape
    nc = S // lb
    l = SSD_CHUNK
    assert lb == l or nc == 1
    heads, hdim, nstate = wts["heads"], wts["hdim"], wts["nstate"]
    inner = heads * hdim
    kconv = wts["conv_w"].shape[0]
    row = lambda b, c: (b * nc + c, 0)
    fixed = lambda b, c: (0, 0)
    small = [wts["conv_w"], wts["conv_b"], wts["dtb_row"], wts["dtb_col"], wts["alog_row"], wts["alog_col"],
             wts["dskip"], wts["gy"]]
    return pl.pallas_call(
        functools.partial(_ssd_kernel, l=l, valid=valid, heads=heads, hdim=hdim, nstate=nstate, kconv=kconv),
        grid=(B, nc),
        in_specs=[
            pl.BlockSpec((lb, cdim), row),
            pl.BlockSpec((lb, inner), row),
            pl.BlockSpec((lb, dt.shape[1]), row),
            pl.BlockSpec((None, dtT.shape[1], l), lambda b, c: (b, 0, c)),
            pl.BlockSpec((None, None, SUBLANES, cdim), lambda b, c: (hist[1], b, 0, 0)),
            pl.BlockSpec((None, None, inner, nstate), lambda b, c: (h0[1], b, 0, 0)),
        ] + [a.spec() for a in small],
        out_specs=(pl.BlockSpec((lb, inner), row),
                   pl.BlockSpec((None, inner, nstate), lambda b, c: (b, 0, 0))),
        out_shape=(jax.ShapeDtypeStruct((T, inner), act_dtype),
                   jax.ShapeDtypeStruct((B, inner, nstate), F32)),
        scratch_shapes=[pltpu.VMEM((l + 2 * SUBLANES, cdim), F32),
                        pltpu.VMEM((inner, nstate), F32)],
        compiler_params=_cparams(("parallel", "arbitrary")),
        name="ssd",
    )(xbc, z, dt, dtT, hist[0], h0[0], *[a.arr for a in small])


def _outproj_kernel(a1_ref, a2_ref, w1_ref, w2_ref, x_ref, g_ref, b_ref, o_ref, *, alpha):
    m = _dot(a1_ref[...].astype(BF16), w1_ref[...]) + _dot(a2_ref[...].astype(BF16), w2_ref[...])
    o_ref[...] = _layer_norm(alpha * x_ref[...] + m, g_ref[...], b_ref[...])


def _outproj(a1, a2, w1, w2, x, g, b, alpha, tm):
    T, D = x.shape
    row = lambda i: (i, 0)
    return pl.pallas_call(
        functools.partial(_outproj_kernel, alpha=alpha),
        grid=(T // tm,),
        in_specs=[pl.BlockSpec((tm, a1.shape[1]), row), pl.BlockSpec((tm, a2.shape[1]), row),
                  w1.spec(), w2.spec(), pl.BlockSpec((tm, D), row), g.spec(), b.spec()],
        out_specs=pl.BlockSpec((tm, D), row),
        out_shape=jax.ShapeDtypeStruct((T, D), F32),
        compiler_params=_cparams(("parallel",)),
        name="outproj_ln",
    )(a1, a2, w1.arr, w2.arr, x, g.arr, b.arr)


def _conf_kernel(x_ref, hist_ref, wa_ref, wb_ref, ba_ref, bb_ref, dww_ref, dwb_ref, gn_ref, bn_ref, wo_ref, bo_ref,
                 g_ref, b_ref, o_ref, vlast_ref, xp_scr, *, nseq, rows, halo, ktaps, rblk, alpha):
    c = pl.program_id(1)
    nc = pl.num_programs(1)
    nlb = dww_ref.shape[0]
    first = halo - (ktaps - 1)

    @pl.when(c == 0)
    def _():
        for cb in range(nlb):
            xp_scr[cb, :, 0:halo, :] = hist_ref[:, :, cb * LANES:(cb + 1) * LANES]

    x = x_ref[...]
    xb = x.astype(BF16)
    per = 2
    convs = []
    for cc in range(nlb // per):
        sl = slice(cc * per * LANES, (cc + 1) * per * LANES)
        ua = _dot(xb, wa_ref[:, sl]) + ba_ref[:, sl]
        ub = _dot(xb, wb_ref[:, sl]) + bb_ref[:, sl]
        v = ua * jax.nn.sigmoid(ub)
        for h in range(per):
            cb = cc * per + h
            xp_scr[cb, :, halo:halo + rows, :] = v[:, h * LANES:(h + 1) * LANES].reshape(nseq, rows, LANES)
            outs = []
            for s in range(nseq):
                for rb in range(rows // rblk):
                    acc = jnp.zeros((rblk, LANES), F32) + dwb_ref[cb]
                    for k in range(ktaps):
                        acc = acc + dww_ref[cb, k:k + 1, :] * xp_scr[cb, s, pl.ds(rb * rblk + first + k, rblk), :]
                    outs.append(acc)
            convs.append(jnp.concatenate(outs, axis=0))
            tail = xp_scr[cb, :, rows:rows + halo, :]
            xp_scr[cb, :, 0:halo, :] = tail

    @pl.when(c == nc - 1)
    def _():
        for cb in range(nlb):
            vlast_ref[:, :, cb * LANES:(cb + 1) * LANES] = xp_scr[cb, :, halo:halo + rows, :]

    conv = jnp.concatenate(convs, axis=-1)
    cn = _silu(_layer_norm(conv, gn_ref[...], bn_ref[...]))
    m = _dot(cn.astype(BF16), wo_ref[...]) + bo_ref[...]
    o_ref[...] = _layer_norm(alpha * x + m, g_ref[...], b_ref[...])


def _conf(x, hist, wts, g, b, B, S, nseq, rows, alpha):
    T, D = x.shape
    nc = S // rows
    assert nseq == 1 or nc == 1
    halo = hist[0].shape[2]
    nlb, ktaps, _ = wts["dww"].shape
    C = nlb * LANES
    tmr = nseq * rows
    row = lambda bb, c: (bb * nc + c, 0)
    small = [wts[k] for k in ("wa", "wb", "ba", "bb", "dww", "dwb", "gn", "bn", "wo", "bo")] + [g, b]
    return pl.pallas_call(
        functools.partial(_conf_kernel, nseq=nseq, rows=rows, halo=halo, ktaps=ktaps, rblk=min(rows, 64),
                          alpha=alpha),
        grid=(B // nseq, nc),
        in_specs=[pl.BlockSpec((tmr, D), row),
                  pl.BlockSpec((None, nseq, halo, C), lambda bb, c: (hist[1], bb, 0, 0))]
        + [a.spec() for a in small],
        out_specs=(pl.BlockSpec((tmr, D), row), pl.BlockSpec((nseq, rows, C), lambda bb, c: (bb, 0, 0))),
        out_shape=(jax.ShapeDtypeStruct((T, D), F32), jax.ShapeDtypeStruct((B, rows, C), F32)),
        scratch_shapes=[pltpu.VMEM((nlb, nseq, rows + halo, LANES), F32)],
        compiler_params=_cparams(("parallel", "arbitrary")),
        name="conformer",
    )(x, hist[0], *[a.arr for a in small])


def _route(xh, wh, rb):
    n_exp = wh.shape[0]
    logits = _dot_nt(wh, xh)
    probs = jax.nn.sigmoid(logits)
    sel = probs + rb
    per = n_exp // N_GROUPS
    s = [sel[e:e + 1, :] for e in range(n_exp)]
    top = []
    for e in range(n_exp):
        g0 = (e // per) * per
        rank = jnp.zeros(s[e].shape, F32)
        for j in range(g0, g0 + per):
            if j < e:
                rank = rank + jnp.where(s[j] >= s[e], 1.0, 0.0)
            elif j > e:
                rank = rank + jnp.where(s[j] > s[e], 1.0, 0.0)
        top.append(rank < TOP_K)
    grp = []
    for g in range(N_GROUPS):
        tot = jnp.zeros(s[0].shape, F32)
        for e in range(g * per, (g + 1) * per):
            tot = tot + jnp.where(top[e], s[e], 0.0)
        grp.append(tot)
    chosen = []
    for g in range(N_GROUPS):
        ok = None
        for j in range(N_GROUPS):
            if j == g:
                continue
            t = (grp[j] < grp[g]) if j < g else (grp[j] <= grp[g])
            ok = t if ok is None else (ok & t)
        chosen.append(ok)
    w = [jnp.where(chosen[e // per] & top[e], probs[e:e + 1, :], 0.0) for e in range(n_exp)]
    den = w[0]
    for e in range(1, n_exp):
        den = den + w[e]
    return [w[e] / den for e in range(n_exp)]


def _moe_kernel(x_ref, wh_ref, rb_ref, wg_ref, wu_ref, wd_ref, g_ref, b_ref, o_ref,
                xb_scr, acc_scr, gt_scr, gates_scr, *, alpha):
    grp = pl.program_id(1)
    ngrp = pl.num_programs(1)
    per, F, D = wd_ref.shape

    @pl.when(grp == 0)
    def _():
        xh = x_ref[...].astype(BF16)
        xb_scr[...] = xh
        acc_scr[...] = jnp.zeros(acc_scr.shape, F32)
        gt_scr[...] = jnp.zeros(gt_scr.shape, F32)
        for e, row in enumerate(_route(xh, wh_ref[...], rb_ref[...])):
            gt_scr[e:e + 1, :] = row
        gates_scr[...] = gt_scr[...].T

    xb = xb_scr[...]
    gates = gates_scr[...]
    lane = lax.broadcasted_iota(jnp.int32, gates.shape, 1)
    hs = []
    for j in range(per):
        gate = jnp.sum(jnp.where(lane == grp * per + j, gates, 0.0), axis=-1, keepdims=True)
        h = _silu(_dot(xb, wg_ref[j].astype(BF16))) * _dot(xb, wu_ref[j].astype(BF16))
        hs.append((h * gate).astype(BF16))
    acc_scr[...] += _dot(jnp.concatenate(hs, axis=-1), wd_ref[...].reshape(per * F, D).astype(BF16))

    @pl.when(grp == ngrp - 1)
    def _():
        o_ref[...] = _layer_norm(alpha * x_ref[...] + acc_scr[...], g_ref[...], b_ref[...])


def _moe(x, rt, wg, wu, wd, g, b, layer, alpha, tm):
    T, D = x.shape
    _, E, _, F = wg.shape
    per = E // N_GROUPS
    assert E <= LANES
    row = lambda i, e: (i, 0)
    return pl.pallas_call(
        functools.partial(_moe_kernel, alpha=alpha),
        grid=(T // tm, N_GROUPS),
        in_specs=[pl.BlockSpec((tm, D), row), rt["wh"].spec(), rt["rb"].spec(),
                  pl.BlockSpec((None, per, D, F), lambda i, e: (layer, e, 0, 0)),
                  pl.BlockSpec((None, per, D, F), lambda i, e: (layer, e, 0, 0)),
                  pl.BlockSpec((None, per, F, D), lambda i, e: (layer, e, 0, 0)),
                  g.spec(), b.spec()],
        out_specs=pl.BlockSpec((tm, D), row),
        out_shape=jax.ShapeDtypeStruct((T, D), F32),
        scratch_shapes=[pltpu.VMEM((tm, D), BF16), pltpu.VMEM((tm, D), F32),
                        pltpu.VMEM((LANES, tm), F32), pltpu.VMEM((tm, LANES), F32)],
        compiler_params=_cparams(("parallel", "arbitrary")),
        name="moe",
    )(x, rt["wh"].arr, rt["rb"].arr, wg, wu, wd, g.arr, b.arr)


def _rope_tables(pos, heads, reps):
    half = ROPE_DIM // 2
    inv = ROPE_BASE ** (-jnp.arange(half, dtype=F32) / half)
    ang = pos.astype(F32)[:, None] * inv[None, :]
    cos, sin = jnp.cos(ang), jnp.sin(ang)
    cos_t = jnp.tile(jnp.concatenate([cos, cos], -1), (reps, heads))
    sin_t = jnp.tile(jnp.concatenate([-sin, sin], -1), (reps, heads))
    return cos_t, sin_t


def _swap_halves(w):
    half = ROPE_DIM // 2
    return jnp.concatenate([w[..., half:], w[..., :half]], axis=-1)


def _prep_ab(w_in, g_q, g_kv, w_uq, w_uk, w_uv, conv_w, conv_b, dt_bias, a_log, d_skip, g_y, w_out):
    n_a, q_lora = g_q.shape
    kv_lora = g_kv.shape[1]
    heads = w_uq.shape[2]
    s_heads = dt_bias.shape[1]
    cdim = conv_w.shape[2]
    o1 = q_lora
    o2 = o1 + kv_lora
    o3 = o2 + ROPE_DIM
    inner = w_in.shape[2] - o3 - cdim - s_heads
    o4 = o3 + inner
    o5 = o4 + cdim
    hdim = inner // s_heads
    nstate = (cdim - inner) // (2 * SSM_GROUPS)
    wkp = w_in[:, :, o2:o3]
    dt_pad = jnp.pad(w_in[:, :, o5:], ((0, 0), (0, 0), (0, LANES - s_heads)))
    pieces = [("q", w_in[:, :, :o1]), ("kv", w_in[:, :, o1:o2]), ("kpa", jnp.tile(wkp, (1, 1, heads))),
              ("kpb", jnp.tile(_swap_halves(wkp), (1, 1, heads))), ("z", w_in[:, :, o3:o4]),
              ("xbc", w_in[:, :, o4:o5]), ("dt", dt_pad)]
    cols, off = {}, 0
    for name, p in pieces:
        cols[name] = (off, off + p.shape[2])
        off += p.shape[2]
    w_all = jnp.concatenate([p for _, p in pieces], axis=2).astype(BF16)

    wn = w_uq[..., :NOPE_DIM].reshape(n_a, q_lora, heads * NOPE_DIM).astype(BF16)
    wpe = w_uq[..., NOPE_DIM:]
    wpa = wpe.reshape(n_a, q_lora, heads * ROPE_DIM).astype(BF16)
    wpb = _swap_halves(wpe).reshape(n_a, q_lora, heads * ROPE_DIM).astype(BF16)
    per = LANES // NOPE_DIM
    slot = (jnp.arange(heads)[:, None] % per == jnp.arange(per)[None, :]).astype(F32)
    ukt = jnp.transpose(w_uk, (0, 2, 3, 1))
    wuk = (ukt[:, :, None] * slot[None, :, :, None, None]).reshape(n_a, heads, LANES, kv_lora).astype(BF16)
    v_dim = w_uv.shape[3]
    vper = LANES // v_dim
    vslot = (jnp.arange(heads)[:, None] % vper == jnp.arange(vper)[None, :]).astype(F32)
    uvt = jnp.transpose(w_uv, (0, 2, 1, 3))
    wuv = (uvt[:, :, :, None] * vslot[None, :, None, :, None]).reshape(n_a, heads, kv_lora, LANES).astype(BF16)

    def row(v, width):
        return jnp.pad(v, ((0, 0), (0, width - v.shape[1])))[:, None, :]

    def col(v, height):
        return jnp.pad(v, ((0, 0), (0, height - v.shape[1])))[:, :, None]

    ssd = dict(conv_w=conv_w, conv_b=conv_b[:, None, :],
               dtb_row=row(dt_bias, LANES), dtb_col=col(dt_bias, 2 * SUBLANES),
               alog_row=row(a_log, LANES), alog_col=col(a_log, 2 * SUBLANES),
               dskip=jnp.repeat(d_skip, hdim, axis=1)[:, None, :], gy=g_y[:, None, :])
    return dict(cols=cols, w_all=w_all, gq=g_q[:, None, :], gkv=g_kv[:, None, :], wn=wn, wpa=wpa, wpb=wpb,
                wuk=wuk, wuv=wuv, ssd=ssd, w_out=w_out.astype(BF16), n_mla=heads * v_dim,
                dims=dict(heads=s_heads, hdim=hdim, nstate=nstate))


def _ab_layer(st, i):
    lay = lambda name: _layer(st[name], i)
    n_out = st["w_out"].shape[1]
    assert 2 * st["n_mla"] == n_out
    ssd = dict(st["dims"], **{k: _layer(v, i) for k, v in st["ssd"].items()})
    return dict(cols=st["cols"], w_all=lay("w_all"), gq=lay("gq"), gkv=lay("gkv"), wn=lay("wn"), wpa=lay("wpa"),
                wpb=lay("wpb"), wuk=lay("wuk"), wuv=lay("wuv"), ssd=ssd,
                w_out_a=_layer(st["w_out"], i, axis=1, part=0, nparts=2),
                w_out_b=_layer(st["w_out"], i, axis=1, part=1, nparts=2))


def _mixer_ab(x, grp, wts, tabs, ln_g, ln_b, alpha, paged=None):
    B, S, tm, l, valid, act = grp["B"], grp["S"], grp["tm"], grp["l"], grp["valid"], grp["act"]
    cos_t, sin_t = tabs
    hq, ckv, kpe, kcat, z, xbc, dt = _inproj(x, wts["w_all"], wts["gq"], wts["gkv"], cos_t, sin_t,
                                            wts["cols"], tm, act)
    qabs, qpe = _qprep(hq, wts["wn"], wts["wpa"], wts["wpb"], wts["wuk"], cos_t, sin_t, tm, act)
    scale = (NOPE_DIM + ROPE_DIM) ** -0.5
    if paged is None:
        o_mla = _attn_prompt(qabs, qpe, kcat, wts["wuv"], B, S, grp["tq"], grp["tk"], scale, act)
    else:
        cache_ckv, cache_kpe, page_table, idx = paged
        o_mla = _attn_sample(qabs, qpe, kcat, wts["wuv"], cache_ckv, cache_kpe, page_table, idx, S, valid, scale)
    s_heads = wts["ssd"]["heads"]
    dtT = jnp.swapaxes(dt[:, :s_heads].reshape(B, S, s_heads), 1, 2)
    dtT = jnp.pad(dtT, ((0, 0), (0, 2 * SUBLANES - s_heads), (0, max(SSD_CHUNK - S, 0))))
    y, h_last = _ssd(xbc, z, dt, dtT, grp["ssm_hist"], grp["ssm_h0"], wts["ssd"], B, S, l, valid, act)
    x_new = _outproj(o_mla, y, wts["w_out_a"], wts["w_out_b"], x, ln_g, ln_b, alpha, tm)
    return x_new, ckv, kpe, h_last, xbc


def _mixer_c(x, grp, wts, ln_g, ln_b, alpha):
    return _conf(x, grp["conf_hist"], wts, ln_g, ln_b, grp["B"], grp["S"], grp["conf_nseq"], grp["conf_rows"], alpha)


def kernel(x_prompt, x_sample, cache_ckv, cache_kpe, state_ssm, state_ssm_conv, state_conf_conv, page_table,
           w_in_ab, g_q_norm, g_kv_norm, w_uq, w_uk, w_uv, ssm_conv_w, ssm_conv_b, ssm_dt_bias, ssm_a_log,
           ssm_d, ssm_norm_g, w_out_ab, conf_w_in, conf_b_in, conf_dw_w, conf_dw_b, conf_norm_g, conf_norm_b,
           conf_w_out, conf_b_out, ln_mix_g, ln_mix_b, ln_ffn_g, ln_ffn_b, w_router, router_bias,
           moe_w_gate, moe_w_up, moe_w_down):
    bp, S, D = x_prompt.shape
    bd, t_new, _ = x_sample.shape
    depth = ln_mix_g.shape[0]
    past_len = page_table.shape[1] * PAGE_SIZE
    alpha = (2 * depth) ** 0.25
    heads = w_uq.shape[2]
    n_a = w_in_ab.shape[0]
    n_c = conf_w_in.shape[0]
    conf_ch = conf_dw_w.shape[2]
    conf_k = conf_dw_w.shape[1]
    kconv = ssm_conv_w.shape[1]
    cdim = ssm_conv_w.shape[2]
    rows = SAMPLE_ROWS
    tm_p = 512

    ab_st = _prep_ab(w_in_ab, g_q_norm, g_kv_norm, w_uq, w_uk, w_uv, ssm_conv_w, ssm_conv_b, ssm_dt_bias,
                     ssm_a_log, ssm_d, ssm_norm_g, w_out_ab)
    ab = [_ab_layer(ab_st, i) for i in range(n_a)]
    nlb = conf_ch // LANES
    cw_in = conf_w_in.astype(BF16)
    cb_in = conf_b_in[:, None, :]
    cw_dw = jnp.transpose(conf_dw_w.reshape(n_c, conf_k, nlb, LANES), (0, 2, 1, 3))
    cb_dw = conf_dw_b.reshape(n_c, nlb, 1, LANES)
    cw_out = conf_w_out.astype(BF16)
    c_gn, c_bn, cb_out = conf_norm_g[:, None, :], conf_norm_b[:, None, :], conf_b_out[:, None, :]
    cw = [dict(wa=_layer(cw_in, i, axis=2, part=0, nparts=2), wb=_layer(cw_in, i, axis=2, part=1, nparts=2),
               ba=_layer(cb_in, i, axis=2, part=0, nparts=2), bb=_layer(cb_in, i, axis=2, part=1, nparts=2),
               dww=_layer(cw_dw, i), dwb=_layer(cb_dw, i), gn=_layer(c_gn, i), bn=_layer(c_bn, i),
               wo=_layer(cw_out, i), bo=_layer(cb_out, i))
          for i in range(n_c)]
    wr_t = w_router.T
    wr_h = wr_t.astype(BF16)
    rt = dict(wh=_W(wr_h), rb=_W(router_bias[:, None]))
    wg_b, wu_b, wd_b = moe_w_gate, moe_w_up, moe_w_down
    ln_mg, ln_mb = ln_mix_g[:, None, :], ln_mix_b[:, None, :]
    ln_fg, ln_fb = ln_ffn_g[:, None, :], ln_ffn_b[:, None, :]

    halo = 4 * SUBLANES
    sub_heads = ssm_dt_bias.shape[1]
    inner = state_ssm.shape[2] * state_ssm.shape[3]
    nstate = state_ssm.shape[4]
    gp = dict(B=bp, S=S, tm=tm_p, tq=256, tk=512, l=min(SSD_CHUNK, S), valid=min(SSD_CHUNK, S), act=BF16,
              ssm_hist=(jnp.zeros((1, bp, SUBLANES, cdim), F32), 0),
              ssm_h0=(jnp.zeros((1, bp, inner, nstate), F32), 0),
              conf_hist=(jnp.zeros((1, bp, halo, conf_ch), F32), 0), conf_nseq=1, conf_rows=min(256, S))
    ssm_hist_s = jnp.pad(state_ssm_conv, ((0, 0), (0, 0), (SUBLANES - (kconv - 1), 0), (0, 0)))
    ssm_h0_s = state_ssm.reshape(n_a, bd, inner, nstate)
    conf_hist_s = jnp.pad(state_conf_conv, ((0, 0), (0, 0), (halo - (conf_k - 1), 0), (0, 0)))
    gs = dict(B=bd, S=rows, tm=bd * rows, l=rows, valid=t_new, act=F32, conf_nseq=bd, conf_rows=rows)
    tabs_p = _rope_tables(jnp.arange(S, dtype=jnp.int32), heads, 1)
    pos_s = past_len + jnp.arange(rows, dtype=jnp.int32)
    tabs_s = _rope_tables(pos_s, heads, bd)

    hp = x_prompt.reshape(bp * S, D)
    hs = jnp.pad(x_sample, ((0, 0), (0, rows - t_new), (0, 0))).reshape(bd * rows, D)
    out_p = dict(ckv=[], kpe=[], ssm=[], sconv=[], cconv=[])
    out_s = dict(ckv=[], kpe=[], ssm=[], sconv=[], cconv=[])
    for layer in range(depth):
        i = layer // 2
        lg, lb = _layer(ln_mg, layer), _layer(ln_mb, layer)
        if layer % 2 == 0:
            hp, c1, k1, h1, xbc1 = _mixer_ab(hp, gp, ab[i], tabs_p, lg, lb, alpha)
            gs_l = dict(gs, ssm_hist=(ssm_hist_s, i), ssm_h0=(ssm_h0_s, i))
            hs, c2, k2, h2, xbc2 = _mixer_ab(hs, gs_l, ab[i], tabs_s, lg, lb, alpha,
                                             paged=(cache_ckv, cache_kpe, page_table, i))
            out_p["ckv"].append(c1.reshape(bp, S, -1))
            out_p["kpe"].append(k1.reshape(bp, S, -1)[..., :ROPE_DIM])
            out_p["ssm"].append(h1.reshape(bp, sub_heads, -1, nstate))
            out_p["sconv"].append(xbc1.reshape(bp, S, cdim)[:, S - (kconv - 1):])
            out_s["ckv"].append(c2.reshape(bd, rows, -1)[:, :t_new])
            out_s["kpe"].append(k2.reshape(bd, rows, -1)[:, :t_new, :ROPE_DIM])
            out_s["ssm"].append(h2.reshape(bd, sub_heads, -1, nstate))
            xp = jnp.concatenate([state_ssm_conv[i], xbc2.reshape(bd, rows, cdim)[:, :t_new]], axis=1)
            out_s["sconv"].append(xp[:, xp.shape[1] - (kconv - 1):])
        else:
            hp, v1 = _mixer_c(hp, gp, cw[i], lg, lb, alpha)
            gs_l = dict(gs, conf_hist=(conf_hist_s, i))
            hs, v2 = _mixer_c(hs, gs_l, cw[i], lg, lb, alpha)
            out_p["cconv"].append(v1[:, v1.shape[1] - (conf_k - 1):])
            vp = jnp.concatenate([state_conf_conv[i], v2[:, :t_new]], axis=1)
            out_s["cconv"].append(vp[:, vp.shape[1] - (conf_k - 1):])
        fg, fb = _layer(ln_fg, layer), _layer(ln_fb, layer)
        hp = _moe(hp, rt, wg_b, wu_b, wd_b, fg, fb, layer, alpha, 1024)
        hs = _moe(hs, rt, wg_b, wu_b, wd_b, fg, fb, layer, alpha, bd * rows)
    y_p = hp.reshape(bp, S, D)
    y_s = hs.reshape(bd, rows, D)[:, :t_new]
    return (y_p, y_s,
            jnp.stack(out_p["ckv"]), jnp.stack(out_p["kpe"]), jnp.stack(out_p["ssm"]),
            jnp.stack(out_p["sconv"]), jnp.stack(out_p["cconv"]),
            jnp.stack(out_s["ckv"]), jnp.stack(out_s["kpe"]), jnp.stack(out_s["ssm"]),
            jnp.stack(out_s["sconv"]), jnp.stack(out_s["cconv"]))
```

```python
import functools
import math

import jax
import jax.numpy as jnp
from jax import lax
from jax.experimental import pallas as pl
from jax.experimental.pallas import tpu as pltpu

F32 = jnp.float32
BF16 = jnp.bfloat16

PAGE_SIZE = 128
NOPE_DIM = 64
ROPE_DIM = 32
ROPE_BASE = 10000.0
SSM_GROUPS = 2
SSD_CHUNK = 128
N_GROUPS = 4
TOP_K = 2
EPS = 1e-6

LANES = 128
SUBLANES = 8
VMEM_LIMIT_BYTES = 56 * 1024 * 1024
SAMPLE_ROWS = SUBLANES
PAGES_PER_STEP = 32
SAMPLE_CHAINS = 2


def _dot(a, b):
    return jnp.dot(a, b, preferred_element_type=F32)


def _dot_nt(a, b):
    return lax.dot_general(a, b, (((1,), (1,)), ((), ())), preferred_element_type=F32)


def _dot_tn(a, b):
    return lax.dot_general(a, b, (((0,), (0,)), ((), ())), preferred_element_type=F32)


def _rms(x):
    return x * lax.rsqrt(jnp.mean(x * x, axis=-1, keepdims=True) + EPS)


def _layer_norm(r, g, b):
    mu = jnp.mean(r, axis=-1, keepdims=True)
    d = r - mu
    var = jnp.mean(d * d, axis=-1, keepdims=True)
    return d * lax.rsqrt(var + EPS) * g + b


def _silu(x):
    return x * jax.nn.sigmoid(x)


def _softplus(x):
    return jnp.maximum(x, 0.0) + jnp.log1p(jnp.exp(-jnp.abs(x)))


def _split3(x):
    hi = x.astype(BF16)
    r1 = x - hi.astype(F32)
    mid = r1.astype(BF16)
    lo = (r1 - mid.astype(F32)).astype(BF16)
    return hi, mid, lo


def _cparams(sem):
    return pltpu.CompilerParams(dimension_semantics=sem, vmem_limit_bytes=VMEM_LIMIT_BYTES)


class _W:
    def __init__(self, arr, block=None, index=None):
        self.arr = arr
        self.block = tuple(arr.shape) if block is None else tuple(block)
        self.index = (0,) * arr.ndim if index is None else tuple(index)
        self.shape = tuple(b for b in self.block if b is not None)

    def spec(self):
        index = self.index
        return pl.BlockSpec(self.block, lambda *_: index)


def _layer(arr, i, axis=None, part=0, nparts=1):
    block = [None] + list(arr.shape[1:])
    index = [i] + [0] * (arr.ndim - 1)
    if axis is not None:
        block[axis] = arr.shape[axis] // nparts
        index[axis] = part
    return _W(arr, block, index)


def _inproj_kernel(x_ref, w_ref, gq_ref, gkv_ref, cos_ref, sin_ref,
                   hq_ref, ckv_ref, kpe_ref, kcat_ref, z_ref, xbc_ref, dt_ref, *, cols):
    xb = x_ref[...].astype(BF16)

    def proj(name):
        a, b = cols[name]
        return _dot(xb, w_ref[:, a:b])

    hq_ref[...] = (_rms(proj("q")) * gq_ref[...]).astype(hq_ref.dtype)
    ckv = _rms(proj("kv")) * gkv_ref[...]
    ckv_ref[...] = ckv
    kpe = proj("kpa") * cos_ref[...] + proj("kpb") * sin_ref[...]
    kpe_ref[...] = kpe
    nkv = ckv.shape[-1]
    kcat_ref[:, :nkv] = ckv.astype(kcat_ref.dtype)
    kcat_ref[:, nkv:] = kpe.astype(kcat_ref.dtype)
    z_ref[...] = proj("z")
    xbc_ref[...] = proj("xbc")
    dt_ref[...] = proj("dt")


def _inproj(x, w, gq, gkv, cos_t, sin_t, cols, tm, act_dtype):
    T, D = x.shape
    n_tab = cos_t.shape[0] // tm
    widths = {k: b - a for k, (a, b) in cols.items()}
    row = lambda i: (i, 0)
    tab = lambda i: (i % n_tab, 0)
    out_shapes = (
        jax.ShapeDtypeStruct((T, widths["q"]), act_dtype),
        jax.ShapeDtypeStruct((T, widths["kv"]), F32),
        jax.ShapeDtypeStruct((T, widths["kpa"]), F32),
        jax.ShapeDtypeStruct((T, widths["kv"] + widths["kpa"]), act_dtype),
        jax.ShapeDtypeStruct((T, widths["z"]), F32),
        jax.ShapeDtypeStruct((T, widths["xbc"]), F32),
        jax.ShapeDtypeStruct((T, widths["dt"]), F32),
    )
    return pl.pallas_call(
        functools.partial(_inproj_kernel, cols=cols),
        grid=(T // tm,),
        in_specs=[
            pl.BlockSpec((tm, D), row),
            w.spec(), gq.spec(), gkv.spec(),
            pl.BlockSpec((tm, cos_t.shape[1]), tab),
            pl.BlockSpec((tm, sin_t.shape[1]), tab),
        ],
        out_specs=tuple(pl.BlockSpec((tm, s.shape[1]), row) for s in out_shapes),
        out_shape=out_shapes,
        compiler_params=_cparams(("parallel",)),
        name="inproj",
    )(x, w.arr, gq.arr, gkv.arr, cos_t, sin_t)


def _qprep_kernel(hq_ref, wn_ref, wpa_ref, wpb_ref, wuk_ref, cos_ref, sin_ref, qabs_ref, qpe_ref, *, heads):
    hq = hq_ref[...].astype(BF16)
    qn = _dot(hq, wn_ref[...])
    qpe = _dot(hq, wpa_ref[...]) * cos_ref[...] + _dot(hq, wpb_ref[...]) * sin_ref[...]
    qpe_ref[...] = qpe.astype(qpe_ref.dtype)
    per = LANES // NOPE_DIM
    for h in range(heads):
        slab = qn[:, (h // per) * LANES:(h // per + 1) * LANES].astype(BF16)
        qabs_ref[h] = _dot(slab, wuk_ref[h]).astype(qabs_ref.dtype)


def _qprep(hq, wn, wpa, wpb, wuk, cos_t, sin_t, tm, act_dtype):
    T, Q = hq.shape
    heads, _, R = wuk.shape
    n_tab = cos_t.shape[0] // tm
    row = lambda i: (i, 0)
    tab = lambda i: (i % n_tab, 0)
    return pl.pallas_call(
        functools.partial(_qprep_kernel, heads=heads),
        grid=(T // tm,),
        in_specs=[
            pl.BlockSpec((tm, Q), row),
            wn.spec(), wpa.spec(), wpb.spec(), wuk.spec(),
            pl.BlockSpec((tm, cos_t.shape[1]), tab),
            pl.BlockSpec((tm, sin_t.shape[1]), tab),
        ],
        out_specs=(pl.BlockSpec((heads, tm, R), lambda i: (0, i, 0)),
                   pl.BlockSpec((tm, wpa.shape[1]), row)),
        out_shape=(jax.ShapeDtypeStruct((heads, T, R), act_dtype),
                   jax.ShapeDtypeStruct((T, wpa.shape[1]), act_dtype)),
        compiler_params=_cparams(("parallel",)),
        name="qprep",
    )(hq, wn.arr, wpa.arr, wpb.arr, wuk.arr, cos_t, sin_t)


def _attn_kernel(qi_ref, kj_ref, qabs_ref, qpe_ref, k_ref, wuv_ref, o_ref,
                 q_scr, m_scr, l_scr, acc_scr, *, heads, tq, tk, scale):
    s = pl.program_id(1)
    i = qi_ref[s]
    j = kj_ref[s]
    nkv = qabs_ref.shape[-1]
    c2 = scale * math.log2(math.e)

    @pl.when(j == 0)
    def _():
        qpe = qpe_ref[...]
        lane = lax.broadcasted_iota(jnp.int32, qpe.shape, 1)
        for h in range(heads):
            q_scr[h * tq:(h + 1) * tq, :nkv] = qabs_ref[h].astype(BF16)
            q_scr[h * tq:(h + 1) * tq, nkv:] = jnp.where(lane // ROPE_DIM == h, qpe, 0).astype(BF16)
        m_scr[...] = jnp.full(m_scr.shape, -jnp.inf, F32)
        l_scr[...] = jnp.zeros(l_scr.shape, F32)
        acc_scr[...] = jnp.zeros(acc_scr.shape, F32)

    k = k_ref[...]

    def update(sc):
        m_old = m_scr[...]
        m_new = jnp.maximum(m_old, jnp.max(sc, axis=-1, keepdims=True))
        alpha = jnp.exp2((m_old - m_new) * c2)
        p = jnp.exp2((sc - jnp.tile(m_new, (1, tk // LANES))) * c2)
        psum = p[:, :LANES]
        for t in range(1, tk // LANES):
            psum = psum + p[:, t * LANES:(t + 1) * LANES]
        l_scr[...] = alpha * l_scr[...] + psum
        acc_scr[...] = jnp.tile(alpha, (1, nkv // LANES)) * acc_scr[...] + _dot(p.astype(BF16), k[:, :nkv])
        m_scr[...] = m_new

    diag = ((i + 1) * tq - 1) // tk

    @pl.when(j < diag)
    def _():
        update(_dot_nt(q_scr[...], k))

    @pl.when(j == diag)
    def _():
        sc = _dot_nt(q_scr[...], k)
        r = lax.broadcasted_iota(jnp.int32, sc.shape, 0)
        c = lax.broadcasted_iota(jnp.int32, sc.shape, 1)
        update(jnp.where(j * tk + c <= i * tq + (r & (tq - 1)), sc, -jnp.inf))
        o = acc_scr[...] / jnp.sum(l_scr[...], axis=-1, keepdims=True)
        for p in range(heads // 2):
            lo = o[(2 * p) * tq:(2 * p + 1) * tq].astype(BF16)
            hi = o[(2 * p + 1) * tq:(2 * p + 2) * tq].astype(BF16)
            o_ref[:, p * LANES:(p + 1) * LANES] = (
                _dot(lo, wuv_ref[2 * p]) + _dot(hi, wuv_ref[2 * p + 1])).astype(o_ref.dtype)


def _attn_prompt(qabs, qpe, kcat, wuv, B, S, tq, tk, scale, act_dtype):
    heads, T, R = qabs.shape
    assert tk % tq == 0 and S % tk == 0
    nq = S // tq
    nk = S // tk
    pairs = [(i, j) for i in range(nq) for j in range(((i + 1) * tq - 1) // tk + 1)]
    qi = jnp.asarray([p[0] for p in pairs], jnp.int32)
    kj = jnp.asarray([p[1] for p in pairs], jnp.int32)
    kw = kcat.shape[1]
    out_w = (heads // 2) * LANES
    grid_spec = pltpu.PrefetchScalarGridSpec(
        num_scalar_prefetch=2,
        grid=(B, len(pairs)),
        in_specs=[
            pl.BlockSpec((heads, tq, R), lambda b, s, qi, kj: (0, b * nq + qi[s], 0)),
            pl.BlockSpec((tq, qpe.shape[1]), lambda b, s, qi, kj: (b * nq + qi[s], 0)),
            pl.BlockSpec((tk, kw), lambda b, s, qi, kj: (b * nk + kj[s], 0)),
            wuv.spec(),
        ],
        out_specs=pl.BlockSpec((tq, out_w), lambda b, s, qi, kj: (b * nq + qi[s], 0)),
        scratch_shapes=[
            pltpu.VMEM((heads * tq, kw), BF16),
            pltpu.VMEM((heads * tq, LANES), F32),
            pltpu.VMEM((heads * tq, LANES), F32),
            pltpu.VMEM((heads * tq, R), F32),
        ],
    )
    return pl.pallas_call(
        functools.partial(_attn_kernel, heads=heads, tq=tq, tk=tk, scale=scale),
        grid_spec=grid_spec,
        out_shape=jax.ShapeDtypeStruct((T, out_w), act_dtype),
        compiler_params=_cparams(("parallel", "arbitrary")),
        name="attn_prompt",
    )(qi, kj, qabs, qpe, kcat, wuv.arr)


def _attn_sample_kernel(pt_ref, qabs_ref, qpe_ref, knew_ref, wuv_ref, ckv_hbm, kpe_hbm, o_ref,
                        qa_scr, qp_scr, m_scr, l_scr, acc_scr, kbuf, pbuf, sem,
                        *, heads, rows, valid, npages, scale, layer):
    b = pl.program_id(0)
    s = pl.program_id(1)
    nb = pl.num_programs(0)
    ns = pl.num_programs(1)
    nkv = qabs_ref.shape[-1]
    page = kbuf.shape[1] // npages
    c2 = scale * math.log2(math.e)
    g = b * ns + s
    slot = g % 2

    def page_copies(bb, ss, sl):
        out = []
        for p in range(npages):
            pg = pt_ref[bb, ss * npages + p]
            out.append(pltpu.make_async_copy(ckv_hbm.at[layer, pg], kbuf.at[sl, pl.ds(p * page, page), :],
                                             sem.at[0, sl]))
            out.append(pltpu.make_async_copy(kpe_hbm.at[layer, pg], pbuf.at[sl, :, pl.ds(p * page, page)],
                                             sem.at[1, sl]))
        return out

    @pl.when(g == 0)
    def _():
        for cp in page_copies(b, s, slot):
            cp.start()

    @pl.when(g + 1 < nb * ns)
    def _():
        gn = g + 1
        for cp in page_copies(gn // ns, gn % ns, 1 - slot):
            cp.start()

    @pl.when(s == 0)
    def _():
        qpe = qpe_ref[...]
        for h in range(heads):
            qa_scr[h * rows:(h + 1) * rows, :] = qabs_ref[h]
            qp_scr[h * rows:(h + 1) * rows, :] = qpe[:, h * ROPE_DIM:(h + 1) * ROPE_DIM]
        m_scr[...] = jnp.full(m_scr.shape, -jnp.inf, F32)
        l_scr[...] = jnp.zeros(l_scr.shape, F32)
        acc_scr[...] = jnp.zeros(acc_scr.shape, F32)

    qa = qa_scr[...].astype(BF16)
    qp = qp_scr[...].astype(BF16)

    def update(st, sc, v):
        m_old = m_scr[st]
        m_new = jnp.maximum(m_old, jnp.max(sc, axis=-1, keepdims=True))
        alpha = jnp.exp2((m_old - m_new) * c2)
        p = jnp.exp2((sc - jnp.tile(m_new, (1, sc.shape[1] // LANES))) * c2)
        psum = p[:, :LANES]
        for t in range(1, sc.shape[1] // LANES):
            psum = psum + p[:, t * LANES:(t + 1) * LANES]
        l_scr[st] = alpha * l_scr[st] + psum
        acc_scr[st] = jnp.tile(alpha, (1, nkv // LANES)) * acc_scr[st] + _dot(p.astype(BF16), v)
        m_scr[st] = m_new

    for cp in page_copies(b, s, slot):
        cp.wait()
    nst = m_scr.shape[0]
    keys = kbuf.shape[1] // nst
    for st in range(nst):
        kb = kbuf[slot, st * keys:(st + 1) * keys, :].astype(BF16)
        pt = pbuf[slot, :, st * keys:(st + 1) * keys].astype(BF16)
        update(st, _dot_nt(qa, kb) + _dot(qp, pt), kb)

    @pl.when(s == ns - 1)
    def _():
        kn = knew_ref[...]
        kn = jnp.concatenate([kn, jnp.zeros((LANES - rows, kn.shape[1]), kn.dtype)], axis=0).astype(BF16)
        kb = kn[:, :nkv]
        sc = _dot_nt(qa, kb) + _dot_nt(qp, kn[:, nkv:nkv + ROPE_DIM])
        tok = lax.broadcasted_iota(jnp.int32, sc.shape, 0) & (rows - 1)
        key = lax.broadcasted_iota(jnp.int32, sc.shape, 1)
        update(0, jnp.where((key <= tok) & (key < valid), sc, -jnp.inf), kb)
        m = m_scr[0]
        for st in range(1, nst):
            m = jnp.maximum(m, m_scr[st])
        lsum = jnp.zeros(m.shape, F32)
        acc = jnp.zeros(acc_scr.shape[1:], F32)
        for st in range(nst):
            w = jnp.exp2((m_scr[st] - m) * c2)
            lsum = lsum + w * l_scr[st]
            acc = acc + jnp.tile(w, (1, nkv // LANES)) * acc_scr[st]
        ob = (acc / jnp.sum(lsum, axis=-1, keepdims=True)).astype(BF16)
        for p in range(heads // 2):
            lo = _dot(ob, wuv_ref[2 * p])[(2 * p) * rows:(2 * p + 1) * rows]
            hi = _dot(ob, wuv_ref[2 * p + 1])[(2 * p + 1) * rows:(2 * p + 2) * rows]
            o_ref[:, p * LANES:(p + 1) * LANES] = (lo + hi).astype(o_ref.dtype)


def _attn_sample(qabs, qpe, knew, wuv, cache_ckv, cache_kpe, page_table, layer, rows, valid, scale):
    heads, T, R = qabs.shape
    B, n_pages = page_table.shape
    npg = PAGES_PER_STEP
    steps = n_pages // npg
    _, _, page, klat = cache_ckv.shape
    out_w = (heads // 2) * LANES
    kpeT = jnp.swapaxes(cache_kpe, 2, 3)

    grid_spec = pltpu.PrefetchScalarGridSpec(
        num_scalar_prefetch=1,
        grid=(B, steps),
        in_specs=[
            pl.BlockSpec((heads, rows, R), lambda b, s, pt: (0, b, 0)),
            pl.BlockSpec((rows, qpe.shape[1]), lambda b, s, pt: (b, 0)),
            pl.BlockSpec((rows, knew.shape[1]), lambda b, s, pt: (b, 0)),
            wuv.spec(),
            pl.BlockSpec(memory_space=pl.ANY),
            pl.BlockSpec(memory_space=pl.ANY),
        ],
        out_specs=pl.BlockSpec((rows, out_w), lambda b, s, pt: (b, 0)),
        scratch_shapes=[
            pltpu.VMEM((heads * rows, R), F32),
            pltpu.VMEM((heads * rows, ROPE_DIM), F32),
            pltpu.VMEM((SAMPLE_CHAINS, heads * rows, LANES), F32),
            pltpu.VMEM((SAMPLE_CHAINS, heads * rows, LANES), F32),
            pltpu.VMEM((SAMPLE_CHAINS, heads * rows, R), F32),
            pltpu.VMEM((2, npg * page, klat), F32),
            pltpu.VMEM((2, ROPE_DIM, npg * page), F32),
            pltpu.SemaphoreType.DMA((2, 2)),
        ],
    )
    return pl.pallas_call(
        functools.partial(_attn_sample_kernel, heads=heads, rows=rows, valid=valid, npages=npg, scale=scale,
                          layer=layer),
        grid_spec=grid_spec,
        out_shape=jax.ShapeDtypeStruct((T, out_w), F32),
        compiler_params=_cparams(("arbitrary", "arbitrary")),
        name="attn_sample",
    )(page_table, qabs, qpe, knew, wuv.arr, cache_ckv, kpeT)


def _ssd_kernel(xbc_ref, z_ref, dt_ref, dtT_ref, hist_ref, h0_ref, cw_ref, cb_ref,
                dtb_row_ref, dtb_col_ref, alog_row_ref, alog_col_ref, dskip_ref, gy_ref,
                y_ref, hlast_ref, xp_scr, h_scr, *, l, valid, heads, hdim, nstate, kconv):
    c = pl.program_id(1)
    nc = pl.num_programs(1)
    inner = heads * hdim
    gw = SSM_GROUPS * nstate
    hpg = heads // SSM_GROUPS
    pad = SUBLANES
    lb = xbc_ref.shape[0]

    def pad_rows(a):
        if lb == l:
            return a
        return jnp.concatenate([a, jnp.zeros((l - lb, a.shape[1]), a.dtype)], axis=0)

    @pl.when(c == 0)
    def _():
        xp_scr[0:pad, :] = hist_ref[...]
        h_scr[...] = h0_ref[...]

    xp_scr[pad:pad + l, :] = pad_rows(xbc_ref[...])
    acc = jnp.zeros((l, xbc_ref.shape[1]), F32) + cb_ref[...]
    for k in range(kconv):
        acc = acc + cw_ref[k:k + 1, :] * xp_scr[pl.ds(pad - (kconv - 1) + k, l), :]
    carry = xp_scr[l:l + pad, :]
    xp_scr[0:pad, :] = carry
    xc = _silu(acc)
    xs = xc[:, :inner]
    bm = xc[:, inner:inner + gw]
    cm = xc[:, inner + gw:inner + 2 * gw]

    dt_col = _softplus(pad_rows(dt_ref[...]) + dtb_row_ref[...])
    dt_row = _softplus(dtT_ref[...] + dtb_col_ref[...])
    if valid < l:
        dt_col = jnp.where(lax.broadcasted_iota(jnp.int32, dt_col.shape, 0) < valid, dt_col, 0.0)
        dt_row = jnp.where(lax.broadcasted_iota(jnp.int32, dt_row.shape, 1) < valid, dt_row, 0.0)
    dta_col = dt_col * (-jnp.exp(alog_row_ref[...]))
    dta_row = dt_row * (-jnp.exp(alog_col_ref[...]))

    ri = lax.broadcasted_iota(jnp.int32, (l, l), 0)
    ci = lax.broadcasted_iota(jnp.int32, (l, l), 1)
    causal = ci <= ri
    lower = jnp.where(causal, 1.0, 0.0).astype(BF16)
    upper = jnp.where(ri <= ci, 1.0, 0.0).astype(BF16)
    acs_col = sum(_dot(lower, piece) for piece in _split3(dta_col))
    acs_row = sum(_dot(piece, upper) for piece in _split3(dta_row))
    acs_last = acs_col[l - 1:l, :]
    eacs = jnp.exp(acs_col)
    toend = jnp.exp(acs_last - acs_col)

    lane = lax.broadcasted_iota(jnp.int32, (l, LANES), 1)
    per = LANES // hdim
    srow = lax.broadcasted_iota(jnp.int32, (LANES, nstate), 0)
    ys = []
    for p in range(heads // per):
        g = (p * per) // hpg
        bg = bm[:, g * nstate:(g + 1) * nstate].astype(BF16)
        cg = cm[:, g * nstate:(g + 1) * nstate]
        cbg = _dot_nt(cg.astype(BF16), bg)
        xs_p = xs[:, p * LANES:(p + 1) * LANES]
        dsk_p = dskip_ref[:, p * LANES:(p + 1) * LANES]
        hprev = h_scr[p * LANES:(p + 1) * LANES, :]
        hprev_b = hprev.astype(BF16)
        dt_p = jnp.zeros((l, LANES), F32)
        te_p = jnp.zeros((l, LANES), F32)
        cd_p = jnp.zeros((LANES, nstate), F32)
        for s in range(per):
            h = p * per + s
            sel = (lane // hdim) == s
            dt_p = jnp.where(sel, dt_col[:, h:h + 1], dt_p)
            te_p = jnp.where(sel, toend[:, h:h + 1], te_p)
            cd_p = jnp.where((srow // hdim) == s, jnp.exp(acs_row[h:h + 1, l - 1:l]), cd_p)
        dtx_p = dt_p * xs_p
        y_p = dsk_p * xs_p
        for s in range(per):
            h = p * per + s
            sel = (lane // hdim) == s
            seg = acs_col[:, h:h + 1] - acs_row[h:h + 1, :]
            dec = jnp.exp(jnp.where(causal, seg, -jnp.inf))
            mh = (cbg * dec).astype(BF16)
            y_h = _dot(mh, jnp.where(sel, dtx_p, 0.0).astype(BF16))
            cs = (cg * eacs[:, h:h + 1]).astype(BF16)
            y_h = y_h + jnp.where(sel, _dot_nt(cs, hprev_b), 0.0)
            y_p = y_p + y_h
        st_p = _dot_tn((dtx_p * te_p).astype(BF16), bg)
        h_scr[p * LANES:(p + 1) * LANES, :] = cd_p * hprev + st_p
        ys.append(y_p)
    y = jnp.concatenate(ys, axis=-1)[:lb] * _silu(z_ref[...])
    gsz = inner // SSM_GROUPS
    outs = [_rms(y[:, g * gsz:(g + 1) * gsz]) for g in range(SSM_GROUPS)]
    y_ref[...] = (jnp.concatenate(outs, axis=-1) * gy_ref[...]).astype(y_ref.dtype)

    @pl.when(c == nc - 1)
    def _():
        hlast_ref[...] = h_scr[...]


def _ssd(xbc, z, dt, dtT, hist, h0, wts, B, S, lb, valid, act_dtype):
    T, cdim = xbc.shape
    nc = S // lb
    l = SSD_CHUNK
    assert lb == l or nc == 1
    heads, hdim, nstate = wts["heads"], wts["hdim"], wts["nstate"]
    inner = heads * hdim
    kconv = wts["conv_w"].shape[0]
    row = lambda b, c: (b * nc + c, 0)
    fixed = lambda b, c: (0, 0)
    small = [wts["conv_w"], wts["conv_b"], wts["dtb_row"], wts["dtb_col"], wts["alog_row"], wts["alog_col"],
             wts["dskip"], wts["gy"]]
    return pl.pallas_call(
        functools.partial(_ssd_kernel, l=l, valid=valid, heads=heads, hdim=hdim, nstate=nstate, kconv=kconv),
        grid=(B, nc),
        in_specs=[
            pl.BlockSpec((lb, cdim), row),
            pl.BlockSpec((lb, inner), row),
            pl.BlockSpec((lb, dt.shape[1]), row),
            pl.BlockSpec((None, dtT.shape[1], l), lambda b, c: (b, 0, c)),
            pl.BlockSpec((None, None, SUBLANES, cdim), lambda b, c: (hist[1], b, 0, 0)),
            pl.BlockSpec((None, None, inner, nstate), lambda b, c: (h0[1], b, 0, 0)),
        ] + [a.spec() for a in small],
        out_specs=(pl.BlockSpec((lb, inner), row),
                   pl.BlockSpec((None, inner, nstate), lambda b, c: (b, 0, 0))),
        out_shape=(jax.ShapeDtypeStruct((T, inner), act_dtype),
                   jax.ShapeDtypeStruct((B, inner, nstate), F32)),
        scratch_shapes=[pltpu.VMEM((l + 2 * SUBLANES, cdim), F32),
                        pltpu.VMEM((inner, nstate), F32)],
        compiler_params=_cparams(("parallel", "arbitrary")),
        name="ssd",
    )(xbc, z, dt, dtT, hist[0], h0[0], *[a.arr for a in small])


def _outproj_kernel(a1_ref, a2_ref, w1_ref, w2_ref, x_ref, g_ref, b_ref, o_ref, *, alpha):
    m = _dot(a1_ref[...].astype(BF16), w1_ref[...]) + _dot(a2_ref[...].astype(BF16), w2_ref[...])
    o_ref[...] = _layer_norm(alpha * x_ref[...] + m, g_ref[...], b_ref[...])


def _outproj(a1, a2, w1, w2, x, g, b, alpha, tm):
    T, D = x.shape
    row = lambda i: (i, 0)
    return pl.pallas_call(
        functools.partial(_outproj_kernel, alpha=alpha),
        grid=(T // tm,),
        in_specs=[pl.BlockSpec((tm, a1.shape[1]), row), pl.BlockSpec((tm, a2.shape[1]), row),
                  w1.spec(), w2.spec(), pl.BlockSpec((tm, D), row), g.spec(), b.spec()],
        out_specs=pl.BlockSpec((tm, D), row),
        out_shape=jax.ShapeDtypeStruct((T, D), F32),
        compiler_params=_cparams(("parallel",)),
        name="outproj_ln",
    )(a1, a2, w1.arr, w2.arr, x, g.arr, b.arr)


def _conf_kernel(x_ref, hist_ref, wa_ref, wb_ref, ba_ref, bb_ref, dww_ref, dwb_ref, gn_ref, bn_ref, wo_ref, bo_ref,
                 g_ref, b_ref, o_ref, vlast_ref, xp_scr, *, nseq, rows, halo, ktaps, rblk, alpha):
    c = pl.program_id(1)
    nc = pl.num_programs(1)
    nlb = dww_ref.shape[0]
    first = halo - (ktaps - 1)

    @pl.when(c == 0)
    def _():
        for cb in range(nlb):
            xp_scr[cb, :, 0:halo, :] = hist_ref[:, :, cb * LANES:(cb + 1) * LANES]

    x = x_ref[...]
    xb = x.astype(BF16)
    per = 2
    convs = []
    for cc in range(nlb // per):
        sl = slice(cc * per * LANES, (cc + 1) * per * LANES)
        ua = _dot(xb, wa_ref[:, sl]) + ba_ref[:, sl]
        ub = _dot(xb, wb_ref[:, sl]) + bb_ref[:, sl]
        v = ua * jax.nn.sigmoid(ub)
        for h in range(per):
            cb = cc * per + h
            xp_scr[cb, :, halo:halo + rows, :] = v[:, h * LANES:(h + 1) * LANES].reshape(nseq, rows, LANES)
            outs = []
            for s in range(nseq):
                for rb in range(rows // rblk):
                    acc = jnp.zeros((rblk, LANES), F32) + dwb_ref[cb]
                    for k in range(ktaps):
                        acc = acc + dww_ref[cb, k:k + 1, :] * xp_scr[cb, s, pl.ds(rb * rblk + first + k, rblk), :]
                    outs.append(acc)
            convs.append(jnp.concatenate(outs, axis=0))
            tail = xp_scr[cb, :, rows:rows + halo, :]
            xp_scr[cb, :, 0:halo, :] = tail

    @pl.when(c == nc - 1)
    def _():
        for cb in range(nlb):
            vlast_ref[:, :, cb * LANES:(cb + 1) * LANES] = xp_scr[cb, :, halo:halo + rows, :]

    conv = jnp.concatenate(convs, axis=-1)
    cn = _silu(_layer_norm(conv, gn_ref[...], bn_ref[...]))
    m = _dot(cn.astype(BF16), wo_ref[...]) + bo_ref[...]
    o_ref[...] = _layer_norm(alpha * x + m, g_ref[...], b_ref[...])


def _conf(x, hist, wts, g, b, B, S, nseq, rows, alpha):
    T, D = x.shape
    nc = S // rows
    assert nseq == 1 or nc == 1
    halo = hist[0].shape[2]
    nlb, ktaps, _ = wts["dww"].shape
    C = nlb * LANES
    tmr = nseq * rows
    row = lambda bb, c: (bb * nc + c, 0)
    small = [wts[k] for k in ("wa", "wb", "ba", "bb", "dww", "dwb", "gn", "bn", "wo", "bo")] + [g, b]
    return pl.pallas_call(
        functools.partial(_conf_kernel, nseq=nseq, rows=rows, halo=halo, ktaps=ktaps, rblk=min(rows, 64),
                          alpha=alpha),
        grid=(B // nseq, nc),
        in_specs=[pl.BlockSpec((tmr, D), row),
                  pl.BlockSpec((None, nseq, halo, C), lambda bb, c: (hist[1], bb, 0, 0))]
        + [a.spec() for a in small],
        out_specs=(pl.BlockSpec((tmr, D), row), pl.BlockSpec((nseq, rows, C), lambda bb, c: (bb, 0, 0))),
        out_shape=(jax.ShapeDtypeStruct((T, D), F32), jax.ShapeDtypeStruct((B, rows, C), F32)),
        scratch_shapes=[pltpu.VMEM((nlb, nseq, rows + halo, LANES), F32)],
        compiler_params=_cparams(("parallel", "arbitrary")),
        name="conformer",
    )(x, hist[0], *[a.arr for a in small])


def _route(xh, wh, rb):
    n_exp = wh.shape[0]
    logits = _dot_nt(wh, xh)
    probs = jax.nn.sigmoid(logits)
    sel = probs + rb
    per = n_exp // N_GROUPS
    s = [sel[e:e + 1, :] for e in range(n_exp)]
    top = []
    for e in range(n_exp):
        g0 = (e // per) * per
        rank = jnp.zeros(s[e].shape, F32)
        for j in range(g0, g0 + per):
            if j < e:
                rank = rank + jnp.where(s[j] >= s[e], 1.0, 0.0)
            elif j > e:
                rank = rank + jnp.where(s[j] > s[e], 1.0, 0.0)
        top.append(rank < TOP_K)
    grp = []
    for g in range(N_GROUPS):
        tot = jnp.zeros(s[0].shape, F32)
        for e in range(g * per, (g + 1) * per):
            tot = tot + jnp.where(top[e], s[e], 0.0)
        grp.append(tot)
    chosen = []
    for g in range(N_GROUPS):
        ok = None
        for j in range(N_GROUPS):
            if j == g:
                continue
            t = (grp[j] < grp[g]) if j < g else (grp[j] <= grp[g])
            ok = t if ok is None else (ok & t)
        chosen.append(ok)
    w = [jnp.where(chosen[e // per] & top[e], probs[e:e + 1, :], 0.0) for e in range(n_exp)]
    den = w[0]
    for e in range(1, n_exp):
        den = den + w[e]
    return [w[e] / den for e in range(n_exp)]


def _moe_kernel(x_ref, wh_ref, rb_ref, wg_ref, wu_ref, wd_ref, g_ref, b_ref, o_ref,
                xb_scr, acc_scr, gt_scr, gates_scr, *, alpha):
    grp = pl.program_id(1)
    ngrp = pl.num_programs(1)
    per, F, D = wd_ref.shape

    @pl.when(grp == 0)
    def _():
        xh = x_ref[...].astype(BF16)
        xb_scr[...] = xh
        acc_scr[...] = jnp.zeros(acc_scr.shape, F32)
        gt_scr[...] = jnp.zeros(gt_scr.shape, F32)
        for e, row in enumerate(_route(xh, wh_ref[...], rb_ref[...])):
            gt_scr[e:e + 1, :] = row
        gates_scr[...] = gt_scr[...].T

    xb = xb_scr[...]
    gates = gates_scr[...]
    lane = lax.broadcasted_iota(jnp.int32, gates.shape, 1)
    hs = []
    for j in range(per):
        gate = jnp.sum(jnp.where(lane == grp * per + j, gates, 0.0), axis=-1, keepdims=True)
        h = _silu(_dot(xb, wg_ref[j].astype(BF16))) * _dot(xb, wu_ref[j].astype(BF16))
        hs.append((h * gate).astype(BF16))
    acc_scr[...] += _dot(jnp.concatenate(hs, axis=-1), wd_ref[...].reshape(per * F, D).astype(BF16))

    @pl.when(grp == ngrp - 1)
    def _():
        o_ref[...] = _layer_norm(alpha * x_ref[...] + acc_scr[...], g_ref[...], b_ref[...])


def _moe(x, rt, wg, wu, wd, g, b, layer, alpha, tm):
    T, D = x.shape
    _, E, _, F = wg.shape
    per = E // N_GROUPS
    assert E <= LANES
    row = lambda i, e: (i, 0)
    return pl.pallas_call(
        functools.partial(_moe_kernel, alpha=alpha),
        grid=(T // tm, N_GROUPS),
        in_specs=[pl.BlockSpec((tm, D), row), rt["wh"].spec(), rt["rb"].spec(),
                  pl.BlockSpec((None, per, D, F), lambda i, e: (layer, e, 0, 0)),
                  pl.BlockSpec((None, per, D, F), lambda i, e: (layer, e, 0, 0)),
                  pl.BlockSpec((None, per, F, D), lambda i, e: (layer, e, 0, 0)),
                  g.spec(), b.spec()],
        out_specs=pl.BlockSpec((tm, D), row),
        out_shape=jax.ShapeDtypeStruct((T, D), F32),
        scratch_shapes=[pltpu.VMEM((tm, D), BF16), pltpu.VMEM((tm, D), F32),
                        pltpu.VMEM((LANES, tm), F32), pltpu.VMEM((tm, LANES), F32)],
        compiler_params=_cparams(("parallel", "arbitrary")),
        name="moe",
    )(x, rt["wh"].arr, rt["rb"].arr, wg, wu, wd, g.arr, b.arr)


def _rope_tables(pos, heads, reps):
    half = ROPE_DIM // 2
    inv = ROPE_BASE ** (-jnp.arange(half, dtype=F32) / half)
    ang = pos.astype(F32)[:, None] * inv[None, :]
    cos, sin = jnp.cos(ang), jnp.sin(ang)
    cos_t = jnp.tile(jnp.concatenate([cos, cos], -1), (reps, heads))
    sin_t = jnp.tile(jnp.concatenate([-sin, sin], -1), (reps, heads))
    return cos_t, sin_t


def _swap_halves(w):
    half = ROPE_DIM // 2
    return jnp.concatenate([w[..., half:], w[..., :half]], axis=-1)


def _prep_ab(w_in, g_q, g_kv, w_uq, w_uk, w_uv, conv_w, conv_b, dt_bias, a_log, d_skip, g_y, w_out):
    n_a, q_lora = g_q.shape
    kv_lora = g_kv.shape[1]
    heads = w_uq.shape[2]
    s_heads = dt_bias.shape[1]
    cdim = conv_w.shape[2]
    o1 = q_lora
    o2 = o1 + kv_lora
    o3 = o2 + ROPE_DIM
    inner = w_in.shape[2] - o3 - cdim - s_heads
    o4 = o3 + inner
    o5 = o4 + cdim
    hdim = inner // s_heads
    nstate = (cdim - inner) // (2 * SSM_GROUPS)
    wkp = w_in[:, :, o2:o3]
    dt_pad = jnp.pad(w_in[:, :, o5:], ((0, 0), (0, 0), (0, LANES - s_heads)))
    pieces = [("q", w_in[:, :, :o1]), ("kv", w_in[:, :, o1:o2]), ("kpa", jnp.tile(wkp, (1, 1, heads))),
              ("kpb", jnp.tile(_swap_halves(wkp), (1, 1, heads))), ("z", w_in[:, :, o3:o4]),
              ("xbc", w_in[:, :, o4:o5]), ("dt", dt_pad)]
    cols, off = {}, 0
    for name, p in pieces:
        cols[name] = (off, off + p.shape[2])
        off += p.shape[2]
    w_all = jnp.concatenate([p for _, p in pieces], axis=2).astype(BF16)

    wn = w_uq[..., :NOPE_DIM].reshape(n_a, q_lora, heads * NOPE_DIM).astype(BF16)
    wpe = w_uq[..., NOPE_DIM:]
    wpa = wpe.reshape(n_a, q_lora, heads * ROPE_DIM).astype(BF16)
    wpb = _swap_halves(wpe).reshape(n_a, q_lora, heads * ROPE_DIM).astype(BF16)
    per = LANES // NOPE_DIM
    slot = (jnp.arange(heads)[:, None] % per == jnp.arange(per)[None, :]).astype(F32)
    ukt = jnp.transpose(w_uk, (0, 2, 3, 1))
    wuk = (ukt[:, :, None] * slot[None, :, :, None, None]).reshape(n_a, heads, LANES, kv_lora).astype(BF16)
    v_dim = w_uv.shape[3]
    vper = LANES // v_dim
    vslot = (jnp.arange(heads)[:, None] % vper == jnp.arange(vper)[None, :]).astype(F32)
    uvt = jnp.transpose(w_uv, (0, 2, 1, 3))
    wuv = (uvt[:, :, :, None] * vslot[None, :, None, :, None]).reshape(n_a, heads, kv_lora, LANES).astype(BF16)

    def row(v, width):
        return jnp.pad(v, ((0, 0), (0, width - v.shape[1])))[:, None, :]

    def col(v, height):
        return jnp.pad(v, ((0, 0), (0, height - v.shape[1])))[:, :, None]

    ssd = dict(conv_w=conv_w, conv_b=conv_b[:, None, :],
               dtb_row=row(dt_bias, LANES), dtb_col=col(dt_bias, 2 * SUBLANES),
               alog_row=row(a_log, LANES), alog_col=col(a_log, 2 * SUBLANES),
               dskip=jnp.repeat(d_skip, hdim, axis=1)[:, None, :], gy=g_y[:, None, :])
    return dict(cols=cols, w_all=w_all, gq=g_q[:, None, :], gkv=g_kv[:, None, :], wn=wn, wpa=wpa, wpb=wpb,
                wuk=wuk, wuv=wuv, ssd=ssd, w_out=w_out.astype(BF16), n_mla=heads * v_dim,
                dims=dict(heads=s_heads, hdim=hdim, nstate=nstate))


def _ab_layer(st, i):
    lay = lambda name: _layer(st[name], i)
    n_out = st["w_out"].shape[1]
    assert 2 * st["n_mla"] == n_out
    ssd = dict(st["dims"], **{k: _layer(v, i) for k, v in st["ssd"].items()})
    return dict(cols=st["cols"], w_all=lay("w_all"), gq=lay("gq"), gkv=lay("gkv"), wn=lay("wn"), wpa=lay("wpa"),
                wpb=lay("wpb"), wuk=lay("wuk"), wuv=lay("wuv"), ssd=ssd,
                w_out_a=_layer(st["w_out"], i, axis=1, part=0, nparts=2),
                w_out_b=_layer(st["w_out"], i, axis=1, part=1, nparts=2))


def _mixer_ab(x, grp, wts, tabs, ln_g, ln_b, alpha, paged=None):
    B, S, tm, l, valid, act = grp["B"], grp["S"], grp["tm"], grp["l"], grp["valid"], grp["act"]
    cos_t, sin_t = tabs
    hq, ckv, kpe, kcat, z, xbc, dt = _inproj(x, wts["w_all"], wts["gq"], wts["gkv"], cos_t, sin_t,
                                            wts["cols"], tm, act)
    qabs, qpe = _qprep(hq, wts["wn"], wts["wpa"], wts["wpb"], wts["wuk"], cos_t, sin_t, tm, act)
    scale = (NOPE_DIM + ROPE_DIM) ** -0.5
    if paged is None:
        o_mla = _attn_prompt(qabs, qpe, kcat, wts["wuv"], B, S, grp["tq"], grp["tk"], scale, act)
    else:
        cache_ckv, cache_kpe, page_table, idx = paged
        o_mla = _attn_sample(qabs, qpe, kcat, wts["wuv"], cache_ckv, cache_kpe, page_table, idx, S, valid, scale)
    s_heads = wts["ssd"]["heads"]
    dtT = jnp.swapaxes(dt[:, :s_heads].reshape(B, S, s_heads), 1, 2)
    dtT = jnp.pad(dtT, ((0, 0), (0, 2 * SUBLANES - s_heads), (0, max(SSD_CHUNK - S, 0))))
    y, h_last = _ssd(xbc, z, dt, dtT, grp["ssm_hist"], grp["ssm_h0"], wts["ssd"], B, S, l, valid, act)
    x_new = _outproj(o_mla, y, wts["w_out_a"], wts["w_out_b"], x, ln_g, ln_b, alpha, tm)
    return x_new, ckv, kpe, h_last, xbc


def _mixer_c(x, grp, wts, ln_g, ln_b, alpha):
    return _conf(x, grp["conf_hist"], wts, ln_g, ln_b, grp["B"], grp["S"], grp["conf_nseq"], grp["conf_rows"], alpha)


def kernel(x_prompt, x_sample, cache_ckv, cache_kpe, state_ssm, state_ssm_conv, state_conf_conv, page_table,
           w_in_ab, g_q_norm, g_kv_norm, w_uq, w_uk, w_uv, ssm_conv_w, ssm_conv_b, ssm_dt_bias, ssm_a_log,
           ssm_d, ssm_norm_g, w_out_ab, conf_w_in, conf_b_in, conf_dw_w, conf_dw_b, conf_norm_g, conf_norm_b,
           conf_w_out, conf_b_out, ln_mix_g, ln_mix_b, ln_ffn_g, ln_ffn_b, w_router, router_bias,
           moe_w_gate, moe_w_up, moe_w_down):
    bp, S, D = x_prompt.shape
    bd, t_new, _ = x_sample.shape
    depth = ln_mix_g.shape[0]
    past_len = page_table.shape[1] * PAGE_SIZE
    alpha = (2 * depth) ** 0.25
    heads = w_uq.shape[2]
    n_a = w_in_ab.shape[0]
    n_c = conf_w_in.shape[0]
    conf_ch = conf_dw_w.shape[2]
    conf_k = conf_dw_w.shape[1]
    kconv = ssm_conv_w.shape[1]
    cdim = ssm_conv_w.shape[2]
    rows = SAMPLE_ROWS
    tm_p = 512

    ab_st = _prep_ab(w_in_ab, g_q_norm, g_kv_norm, w_uq, w_uk, w_uv, ssm_conv_w, ssm_conv_b, ssm_dt_bias,
                     ssm_a_log, ssm_d, ssm_norm_g, w_out_ab)
    ab = [_ab_layer(ab_st, i) for i in range(n_a)]
    nlb = conf_ch // LANES
    cw_in = conf_w_in.astype(BF16)
    cb_in = conf_b_in[:, None, :]
    cw_dw = jnp.transpose(conf_dw_w.reshape(n_c, conf_k, nlb, LANES), (0, 2, 1, 3))
    cb_dw = conf_dw_b.reshape(n_c, nlb, 1, LANES)
    cw_out = conf_w_out.astype(BF16)
    c_gn, c_bn, cb_out = conf_norm_g[:, None, :], conf_norm_b[:, None, :], conf_b_out[:, None, :]
    cw = [dict(wa=_layer(cw_in, i, axis=2, part=0, nparts=2), wb=_layer(cw_in, i, axis=2, part=1, nparts=2),
               ba=_layer(cb_in, i, axis=2, part=0, nparts=2), bb=_layer(cb_in, i, axis=2, part=1, nparts=2),
               dww=_layer(cw_dw, i), dwb=_layer(cb_dw, i), gn=_layer(c_gn, i), bn=_layer(c_bn, i),
               wo=_layer(cw_out, i), bo=_layer(cb_out, i))
          for i in range(n_c)]
    wr_t = w_router.T
    wr_h = wr_t.astype(BF16)
    rt = dict(wh=_W(wr_h), rb=_W(router_bias[:, None]))
    wg_b, wu_b, wd_b = moe_w_gate, moe_w_up, moe_w_down
    ln_mg, ln_mb = ln_mix_g[:, None, :], ln_mix_b[:, None, :]
    ln_fg, ln_fb = ln_ffn_g[:, None, :], ln_ffn_b[:, None, :]

    halo = 4 * SUBLANES
    sub_heads = ssm_dt_bias.shape[1]
    inner = state_ssm.shape[2] * state_ssm.shape[3]
    nstate = state_ssm.shape[4]
    gp = dict(B=bp, S=S, tm=tm_p, tq=256, tk=512, l=min(SSD_CHUNK, S), valid=min(SSD_CHUNK, S), act=BF16,
              ssm_hist=(jnp.zeros((1, bp, SUBLANES, cdim), F32), 0),
              ssm_h0=(jnp.zeros((1, bp, inner, nstate), F32), 0),
              conf_hist=(jnp.zeros((1, bp, halo, conf_ch), F32), 0), conf_nseq=1, conf_rows=min(256, S))
    ssm_hist_s = jnp.pad(state_ssm_conv, ((0, 0), (0, 0), (SUBLANES - (kconv - 1), 0), (0, 0)))
    ssm_h0_s = state_ssm.reshape(n_a, bd, inner, nstate)
    conf_hist_s = jnp.pad(state_conf_conv, ((0, 0), (0, 0), (halo - (conf_k - 1), 0), (0, 0)))
    gs = dict(B=bd, S=rows, tm=bd * rows, l=rows, valid=t_new, act=F32, conf_nseq=bd, conf_rows=rows)
    tabs_p = _rope_tables(jnp.arange(S, dtype=jnp.int32), heads, 1)
    pos_s = past_len + jnp.arange(rows, dtype=jnp.int32)
    tabs_s = _rope_tables(pos_s, heads, bd)

    hp = x_prompt.reshape(bp * S, D)
    hs = jnp.pad(x_sample, ((0, 0), (0, rows - t_new), (0, 0))).reshape(bd * rows, D)
    out_p = dict(ckv=[], kpe=[], ssm=[], sconv=[], cconv=[])
    out_s = dict(ckv=[], kpe=[], ssm=[], sconv=[], cconv=[])
    for layer in range(depth):
        i = layer // 2
        lg, lb = _layer(ln_mg, layer), _layer(ln_mb, layer)
        if layer % 2 == 0:
            hp, c1, k1, h1, xbc1 = _mixer_ab(hp, gp, ab[i], tabs_p, lg, lb, alpha)
            gs_l = dict(gs, ssm_hist=(ssm_hist_s, i), ssm_h0=(ssm_h0_s, i))
            hs, c2, k2, h2, xbc2 = _mixer_ab(hs, gs_l, ab[i], tabs_s, lg, lb, alpha,
                                             paged=(cache_ckv, cache_kpe, page_table, i))
            out_p["ckv"].append(c1.reshape(bp, S, -1))
            out_p["kpe"].append(k1.reshape(bp, S, -1)[..., :ROPE_DIM])
            out_p["ssm"].append(h1.reshape(bp, sub_heads, -1, nstate))
            out_p["sconv"].append(xbc1.reshape(bp, S, cdim)[:, S - (kconv - 1):])
            out_s["ckv"].append(c2.reshape(bd, rows, -1)[:, :t_new])
            out_s["kpe"].append(k2.reshape(bd, rows, -1)[:, :t_new, :ROPE_DIM])
            out_s["ssm"].append(h2.reshape(bd, sub_heads, -1, nstate))
            xp = jnp.concatenate([state_ssm_conv[i], xbc2.reshape(bd, rows, cdim)[:, :t_new]], axis=1)
            out_s["sconv"].append(xp[:, xp.shape[1] - (kconv - 1):])
        else:
            hp, v1 = _mixer_c(hp, gp, cw[i], lg, lb, alpha)
            gs_l = dict(gs, conf_hist=(conf_hist_s, i))
            hs, v2 = _mixer_c(hs, gs_l, cw[i], lg, lb, alpha)
            out_p["cconv"].append(v1[:, v1.shape[1] - (conf_k - 1):])
            vp = jnp.concatenate([state_conf_conv[i], v2[:, :t_new]], axis=1)
            out_s["cconv"].append(vp[:, vp.shape[1] - (conf_k - 1):])
        fg, fb = _layer(ln_fg, layer), _layer(ln_fb, layer)
        hp = _moe(hp, rt, wg_b, wu_b, wd_b, fg, fb, layer, alpha, 1024)
        hs = _moe(hs, rt, wg_b, wu_b, wd_b, fg, fb, layer, alpha, bd * rows)
    y_p = hp.reshape(bp, S, D)
    y_s = hs.reshape(bd, rows, D)[:, :t_new]
    return (y_p, y_s,
            jnp.stack(out_p["ckv"]), jnp.stack(out_p["kpe"]), jnp.stack(out_p["ssm"]),
            jnp.stack(out_p["sconv"]), jnp.stack(out_p["cconv"]),
            jnp.stack(out_s["ckv"]), jnp.stack(out_s["kpe"]), jnp.stack(out_s["ssm"]),
            jnp.stack(out_s["sconv"]), jnp.stack(out_s["cconv"]))
```
